```python
import math
import jax
import jax.numpy as jnp
from jax import lax
import numpy as np

D_MODEL = 1024
BATCH = 32
SEQ = 256
DEPTH = 4
DEC_BATCH = 4
DEC_SEQ = 4096
PAST_LEN = 512

GRID_W = 64
N_MIXERS = 3
N_SSD_LAYERS = (DEPTH + 2) // N_MIXERS
N_GLA_LAYERS = (DEPTH + 1) // N_MIXERS
N_CONV_LAYERS = DEPTH // N_MIXERS
EPS = 1e-6

D_INNER = 2 * D_MODEL
SSD_HEAD_DIM = 64
SSD_HEADS = D_INNER // SSD_HEAD_DIM
SSD_GROUPS = 4
SSD_STATE = 128
SSD_CONV = 5
SSD_CHUNK = 128
SSD_CONV_DIM = D_INNER + 2 * SSD_GROUPS * SSD_STATE
SSD_IN = D_INNER + SSD_CONV_DIM + 2 * SSD_HEADS

GLA_HEADS = 4
GLA_DK = D_MODEL // 2
GLA_DV = D_MODEL
GLA_HEAD_K = GLA_DK // GLA_HEADS
GLA_HEAD_V = GLA_DV // GLA_HEADS
GLA_GATE_RANK = 16
GLA_GATE_NORM = 16.0
GLA_CHUNK = 64
GLA_IN = 2 * GLA_DK + 2 * GLA_DV + 2 * GLA_GATE_RANK

SC_CONV = 3

N_EXPERTS = 32
TOP_K = 4
D_FF = D_MODEL
SWIGLU_ALPHA = 1.702
SWIGLU_LIMIT = 7.0
MOE_BLOCK = 256

kernel_name = 'hybrid_diffusion_ssd_gla_shortconv_moe_step'


def rms_norm(x, g):
    xf = x.astype(jnp.float32)
    y = xf * lax.rsqrt(jnp.mean(xf * xf, axis=-1, keepdims=True) + EPS)
    return (y * g.astype(jnp.float32)).astype(x.dtype)


def seq_conv(x, w, grid):
    b, l, ch = x.shape
    seg = GRID_W if grid else l
    xs = x.reshape(b * (l // seg), seg, ch)
    y = lax.conv_general_dilated(xs, w[:, None, :].astype(x.dtype), window_strides=(1,), padding='SAME',
                                 dimension_numbers=('NWC', 'WIO', 'NWC'), feature_group_count=ch)
    return y.reshape(b, l, ch)


def to_col_major(x):
    b, l, d = x.shape
    rows = l // GRID_W
    return x.reshape(b, rows, GRID_W, d).swapaxes(1, 2).reshape(b, l, d)


def to_row_major(x):
    b, l, d = x.shape
    rows = l // GRID_W
    return x.reshape(b, GRID_W, rows, d).swapaxes(1, 2).reshape(b, l, d)


def chunk_recurrence(h0, chunk_states, chunk_decay):
    def step(h, inp):
        s, d = inp
        return h * d + s, h
    h_last, h_in = lax.scan(step, h0, (jnp.moveaxis(chunk_states, 1, 0), jnp.moveaxis(chunk_decay, 1, 0)))
    return jnp.moveaxis(h_in, 0, 1), h_last


def ssd_scan(x, dt, a, bm, cm, h0):
    b, l = x.shape[:2]
    L = SSD_CHUNK
    c = l // L
    G = SSD_GROUPS
    K = SSD_HEADS // G
    f32 = jnp.float32
    xs = x.astype(f32).reshape(b, c, L, G, K, SSD_HEAD_DIM)
    dts = dt.reshape(b, c, L, G, K)
    bs = bm.astype(f32).reshape(b, c, L, G, SSD_STATE)
    cs = cm.astype(f32).reshape(b, c, L, G, SSD_STATE)
    a_cum = jnp.cumsum(dts * a.reshape(G, K), axis=2)
    causal = jnp.tril(jnp.ones((L, L), bool))[:, :, None, None]
    decay = jnp.exp(jnp.where(causal, a_cum[:, :, :, None] - a_cum[:, :, None], -jnp.inf))
    cb = jnp.einsum('bcign,bcjgn->bcijg', cs, bs)
    y_intra = jnp.einsum('bcijgk,bcjgkp->bcigkp', cb[..., None] * decay * dts[:, :, None], xs)
    xw = xs * (jnp.exp(a_cum[:, :, -1:] - a_cum) * dts)[..., None]
    states = jnp.einsum('bcjgn,bcjgkp->bcgkpn', bs, xw).reshape(b, c, SSD_HEADS, SSD_HEAD_DIM, SSD_STATE)
    chunk_decay = jnp.exp(a_cum[:, :, -1]).reshape(b, c, SSD_HEADS, 1, 1)
    h_in, h_last = chunk_recurrence(h0.astype(f32), states, chunk_decay)
    h_in = h_in.reshape(b, c, G, K, SSD_HEAD_DIM, SSD_STATE)
    y_inter = jnp.einsum('bcign,bcgkpn->bcigkp', cs, h_in) * jnp.exp(a_cum)[..., None]
    return (y_intra + y_inter).reshape(b, l, SSD_HEADS, SSD_HEAD_DIM), h_last


def ssd_mixer(h, grid, h0, w_in, conv_w, conv_b, dt_bias, a_log, d_skip, norm_g, w_out):
    b, l, _ = h.shape
    f32 = jnp.float32
    proj = h @ w_in
    z = proj[..., :D_INNER]
    xbc = jax.nn.silu(seq_conv(proj[..., D_INNER:D_INNER + SSD_CONV_DIM], conv_w, grid) + conv_b)
    dt = jax.nn.softplus(proj[..., D_INNER + SSD_CONV_DIM:].astype(f32).reshape(b, l, 2, SSD_HEADS)
                         + dt_bias.astype(f32))
    a = -jnp.exp(a_log.astype(f32))
    gn = SSD_GROUPS * SSD_STATE
    x = xbc[..., :D_INNER].reshape(b, l, SSD_HEADS, SSD_HEAD_DIM)
    bm = xbc[..., D_INNER:D_INNER + gn].reshape(b, l, SSD_GROUPS, SSD_STATE)
    cm = xbc[..., D_INNER + gn:].reshape(b, l, SSD_GROUPS, SSD_STATE)
    y_f, h_f = ssd_scan(x, dt[:, :, 0], a[0], bm, cm, h0[:, 0])
    y_b, h_b = ssd_scan(x[:, ::-1], dt[:, ::-1, 1], a[1], bm[:, ::-1], cm[:, ::-1], h0[:, 1])
    y = y_f + y_b[:, ::-1] + d_skip.astype(f32)[:, None] * x.astype(f32)
    y = y.astype(h.dtype).reshape(b, l, D_INNER) * jax.nn.silu(z)
    return rms_norm(y, norm_g) @ w_out, (h_f, h_b)


def gla_scan(q, k, v, gk, s0):
    b, l = q.shape[:2]
    L = GLA_CHUNK
    c = l // L
    q, k, gk = (t.reshape(b, c, L, GLA_HEADS, GLA_HEAD_K) for t in (q, k, gk))
    v = v.astype(jnp.float32).reshape(b, c, L, GLA_HEADS, GLA_HEAD_V)
    g = jnp.cumsum(gk, axis=2)
    g_ref = g[:, :, L // 2:L // 2 + 1]
    att = jnp.einsum('bcihd,bcjhd->bchij', q * jnp.exp(g - g_ref), k * jnp.exp(g_ref - g))
    att = jnp.where(jnp.tril(jnp.ones((L, L), bool)), att, 0.0)
    o_intra = jnp.einsum('bchij,bcjhv->bcihv', att, v)
    g_end = g[:, :, -1:]
    u = jnp.einsum('bcjhd,bcjhv->bchdv', k * jnp.exp(g_end - g), v)
    s_in, s_last = chunk_recurrence(s0.astype(jnp.float32), u, jnp.exp(g_end[:, :, 0])[..., None])
    o_inter = jnp.einsum('bcihd,bchdv->bcihv', q * jnp.exp(g), s_in)
    return (o_intra + o_inter).reshape(b, l, GLA_HEADS, GLA_HEAD_V), s_last


def gla_mixer(h, grid, s0, w_in, w_gate2, b_gate2, norm_g, w_out):
    if grid:
        h = to_col_major(h)
    b, l, _ = h.shape
    f32 = jnp.float32
    proj = h @ w_in
    q = proj[..., :GLA_DK].astype(f32).reshape(b, l, GLA_HEADS, GLA_HEAD_K) * GLA_HEAD_K ** -0.5
    k = proj[..., GLA_DK:2 * GLA_DK].astype(f32).reshape(b, l, GLA_HEADS, GLA_HEAD_K)
    v = proj[..., 2 * GLA_DK:2 * GLA_DK + GLA_DV].reshape(b, l, GLA_HEADS, GLA_HEAD_V)
    r = proj[..., 2 * GLA_DK + GLA_DV:2 * GLA_DK + 2 * GLA_DV]
    g_low = proj[..., 2 * GLA_DK + 2 * GLA_DV:].reshape(b, l, 2, GLA_GATE_RANK)
    g_log = jax.nn.log_sigmoid((jnp.einsum('blzr,zrk->blzk', g_low, w_gate2) + b_gate2).astype(f32)) / GLA_GATE_NORM
    g_log = g_log.reshape(b, l, 2, GLA_HEADS, GLA_HEAD_K)
    o_f, s_f = gla_scan(q, k, v, g_log[:, :, 0], s0[:, 0])
    o_b, s_b = gla_scan(q[:, ::-1], k[:, ::-1], v[:, ::-1], g_log[:, ::-1, 1], s0[:, 1])
    o = rms_norm((o_f + o_b[:, ::-1]).astype(h.dtype), norm_g)
    out = (o.reshape(b, l, GLA_DV) * jax.nn.silu(r)) @ w_out
    if grid:
        out = to_row_major(out)
    return out, (s_f, s_b)


def shortconv_mixer(h, grid, w_in, conv_w, w_out):
    proj = h @ w_in
    gb = proj[..., :D_MODEL]
    gc = proj[..., D_MODEL:2 * D_MODEL]
    u = proj[..., 2 * D_MODEL:]
    return (gb * seq_conv(gc * u, conv_w, grid)) @ w_out


def moe_ffn(h, router_w, router_b, w_gu, b_gu, w_down, b_down):
    b, l, d = h.shape
    n_tok = b * l
    n_as = n_tok * TOP_K
    t = h.reshape(n_tok, d)
    logits = (t @ router_w + router_b).astype(jnp.float32)
    top_val, top_idx = lax.top_k(logits, TOP_K)
    gates = jax.nn.softmax(top_val, axis=-1)
    flat_e = top_idx.reshape(n_as)
    order = jnp.argsort(flat_e)
    sorted_e = flat_e[order]
    counts = jnp.bincount(flat_e, length=N_EXPERTS)
    padded = (counts + MOE_BLOCK - 1) // MOE_BLOCK * MOE_BLOCK
    pad_end = jnp.cumsum(padded)
    grp_start = jnp.cumsum(counts) - counts
    dest = (pad_end - padded)[sorted_e] + jnp.arange(n_as) - grp_start[sorted_e]
    n_blocks = -(-n_as // MOE_BLOCK) + N_EXPERTS
    xb = jnp.zeros((n_blocks * MOE_BLOCK, d), t.dtype).at[dest].set(t[order // TOP_K])
    block_e = jnp.minimum(jnp.searchsorted(pad_end, jnp.arange(n_blocks) * MOE_BLOCK, side='right'),
                          N_EXPERTS - 1)

    def expert_block(args):
        xe, e = args
        gu = xe @ w_gu[e] + b_gu[e]
        g = jnp.minimum(gu[:, :D_FF], SWIGLU_LIMIT)
        u = jnp.clip(gu[:, D_FF:], -SWIGLU_LIMIT, SWIGLU_LIMIT)
        return (g * jax.nn.sigmoid(SWIGLU_ALPHA * g) * (u + 1)) @ w_down[e] + b_down[e]

    yb = lax.map(expert_block, (xb.reshape(n_blocks, MOE_BLOCK, d), block_e)).reshape(n_blocks * MOE_BLOCK, d)
    y_as = jnp.zeros((n_as, d), yb.dtype).at[order].set(yb[dest]).reshape(n_tok, TOP_K, d)
    return jnp.einsum('nkd,nk->nd', y_as, gates.astype(yb.dtype)).reshape(b, l, d)


def trunk(x, cond, grid, ssd_h0, gla_h0, w_mod, b_mod, norm1_g, norm2_g,
          ssd_w_in, ssd_conv_w, ssd_conv_b, ssd_dt_bias, ssd_a_log, ssd_d, ssd_norm_g, ssd_w_out,
          gla_w_in, gla_w_gate2, gla_b_gate2, gla_norm_g, gla_w_out,
          sc_w_in, sc_conv_w, sc_w_out,
          router_w, router_b, moe_w_gu, moe_b_gu, moe_w_down, moe_b_down, final_norm_g):
    cond_act = jax.nn.silu(cond)
    ssd_states, gla_states = [], []
    for i in range(DEPTH):
        mod = (cond_act @ w_mod[i] + b_mod[i])[:, None, :]
        shift1, scale1, gate1, shift2, scale2, gate2 = jnp.split(mod, 6, axis=-1)
        h = rms_norm(x, norm1_g[i]) * (1 + scale1) + shift1
        kind = i % N_MIXERS
        j = i // N_MIXERS
        if kind == 0:
            out, st = ssd_mixer(h, grid, ssd_h0[:, j], ssd_w_in[j], ssd_conv_w[j], ssd_conv_b[j], ssd_dt_bias[j],
                                ssd_a_log[j], ssd_d[j], ssd_norm_g[j], ssd_w_out[j])
            ssd_states.append(st)
        elif kind == 1:
            out, st = gla_mixer(h, grid, gla_h0[:, j], gla_w_in[j], gla_w_gate2[j], gla_b_gate2[j],
                                gla_norm_g[j], gla_w_out[j])
            gla_states.append(st)
        else:
            out = shortconv_mixer(h, grid, sc_w_in[j], sc_conv_w[j], sc_w_out[j])
        x = x + gate1 * out
        h = rms_norm(x, norm2_g[i]) * (1 + scale2) + shift2
        x = x + gate2 * moe_ffn(h, router_w[i], router_b[i], moe_w_gu[i], moe_b_gu[i], moe_w_down[i], moe_b_down[i])
    return rms_norm(x, final_norm_g), ssd_states, gla_states


def setup_inputs(seed: int = 0) -> dict:
    key = jax.random.key(seed)
    ks = iter(jax.random.split(key, 48))
    f32 = jnp.float32
    D = D_MODEL

    def nrm(shape, scale):
        return jax.random.normal(next(ks), shape, f32) * scale

    def gain(shape):
        return 1.0 + nrm(shape, 0.05)

    x_prompt = nrm((BATCH, SEQ, D), 1.0)
    x_sample = nrm((DEC_BATCH, DEC_SEQ, D), 1.0)
    state_ssd = nrm((DEC_BATCH, N_SSD_LAYERS, 2, SSD_HEADS, SSD_HEAD_DIM, SSD_STATE), 0.5)
    state_gla = nrm((DEC_BATCH, N_GLA_LAYERS, 2, GLA_HEADS, GLA_HEAD_K, GLA_HEAD_V), 0.5)
    c = nrm((DEC_BATCH, D), 1.0)
    c_ctx = nrm((D,), 1.0)
    w_mod = nrm((DEPTH, D, 6 * D), 0.5 * D ** -0.5)
    b_mod = nrm((DEPTH, 6 * D), 0.02)
    norm1_g = gain((DEPTH, D))
    norm2_g = gain((DEPTH, D))
    ssd_w_in = nrm((N_SSD_LAYERS, D, SSD_IN), D ** -0.5)
    ssd_conv_w = nrm((N_SSD_LAYERS, SSD_CONV, SSD_CONV_DIM), SSD_CONV ** -0.5)
    ssd_conv_b = nrm((N_SSD_LAYERS, SSD_CONV_DIM), 0.02)
    dt0 = jnp.exp(jax.random.uniform(next(ks), (N_SSD_LAYERS, 2, SSD_HEADS), f32, math.log(1e-3), math.log(1e-1)))
    ssd_dt_bias = dt0 + jnp.log(-jnp.expm1(-dt0))
    ssd_a_log = jnp.log(jax.random.uniform(next(ks), (N_SSD_LAYERS, 2, SSD_HEADS), f32, 1.0, 16.0))
    ssd_d = gain((N_SSD_LAYERS, SSD_HEADS))
    ssd_norm_g = gain((N_SSD_LAYERS, D_INNER))
    ssd_w_out = nrm((N_SSD_LAYERS, D_INNER, D), D_INNER ** -0.5)
    gla_w_in = nrm((N_GLA_LAYERS, D, GLA_IN), D ** -0.5)
    gla_w_gate2 = nrm((N_GLA_LAYERS, 2, GLA_GATE_RANK, GLA_DK), GLA_GATE_RANK ** -0.5)
    gla_b_gate2 = nrm((N_GLA_LAYERS, 2, GLA_DK), 0.1)
    gla_norm_g = gain((N_GLA_LAYERS, GLA_HEAD_V))
    gla_w_out = nrm((N_GLA_LAYERS, GLA_DV, D), GLA_DV ** -0.5)
    sc_w_in = nrm((N_CONV_LAYERS, D, 3 * D), D ** -0.5)
    sc_conv_w = nrm((N_CONV_LAYERS, SC_CONV, D), SC_CONV ** -0.5)
    sc_w_out = nrm((N_CONV_LAYERS, D, D), D ** -0.5)
    router_w = nrm((DEPTH, D, N_EXPERTS), D ** -0.5)
    router_b = nrm((DEPTH, N_EXPERTS), 0.01)
    moe_w_gu = nrm((DEPTH, N_EXPERTS, D, 2 * D_FF), D ** -0.5)
    moe_b_gu = nrm((DEPTH, N_EXPERTS, 2 * D_FF), 0.02)
    moe_w_down = nrm((DEPTH, N_EXPERTS, D_FF, D), D_FF ** -0.5)
    moe_b_down = nrm((DEPTH, N_EXPERTS, D), 0.02)
    final_norm_g = gain((D,))
    return {'x_prompt': x_prompt, 'x_sample': x_sample, 'state_ssd': state_ssd, 'state_gla': state_gla,
            'c': c, 'c_ctx': c_ctx, 'w_mod': w_mod, 'b_mod': b_mod, 'norm1_g': norm1_g, 'norm2_g': norm2_g,
            'ssd_w_in': ssd_w_in, 'ssd_conv_w': ssd_conv_w, 'ssd_conv_b': ssd_conv_b, 'ssd_dt_bias': ssd_dt_bias,
            'ssd_a_log': ssd_a_log, 'ssd_d': ssd_d, 'ssd_norm_g': ssd_norm_g, 'ssd_w_out': ssd_w_out,
            'gla_w_in': gla_w_in, 'gla_w_gate2': gla_w_gate2, 'gla_b_gate2': gla_b_gate2,
            'gla_norm_g': gla_norm_g, 'gla_w_out': gla_w_out,
            'sc_w_in': sc_w_in, 'sc_conv_w': sc_conv_w, 'sc_w_out': sc_w_out,
            'router_w': router_w, 'router_b': router_b, 'moe_w_gu': moe_w_gu, 'moe_b_gu': moe_b_gu,
            'moe_w_down': moe_w_down, 'moe_b_down': moe_b_down, 'final_norm_g': final_norm_g}


def reference(x_prompt, x_sample, state_ssd, state_gla, c, c_ctx, w_mod, b_mod, norm1_g, norm2_g,
              ssd_w_in, ssd_conv_w, ssd_conv_b, ssd_dt_bias, ssd_a_log, ssd_d, ssd_norm_g, ssd_w_out,
              gla_w_in, gla_w_gate2, gla_b_gate2, gla_norm_g, gla_w_out,
              sc_w_in, sc_conv_w, sc_w_out,
              router_w, router_b, moe_w_gu, moe_b_gu, moe_w_down, moe_b_down, final_norm_g):
    weights = (w_mod, b_mod, norm1_g, norm2_g,
               ssd_w_in, ssd_conv_w, ssd_conv_b, ssd_dt_bias, ssd_a_log, ssd_d, ssd_norm_g, ssd_w_out,
               gla_w_in, gla_w_gate2, gla_b_gate2, gla_norm_g, gla_w_out,
               sc_w_in, sc_conv_w, sc_w_out,
               router_w, router_b, moe_w_gu, moe_b_gu, moe_w_down, moe_b_down, final_norm_g)
    nb = x_prompt.shape[0]
    ssd_zero = jnp.zeros((nb, N_SSD_LAYERS, 2, SSD_HEADS, SSD_HEAD_DIM, SSD_STATE), jnp.float32)
    gla_zero = jnp.zeros((nb, N_GLA_LAYERS, 2, GLA_HEADS, GLA_HEAD_K, GLA_HEAD_V), jnp.float32)
    y_prompt, ssd_st, gla_st = trunk(x_prompt, c_ctx[None, :], False, ssd_zero, gla_zero, *weights)
    new_state_ssd = jnp.stack([jnp.stack(p, axis=1) for p in ssd_st], axis=1).astype(x_prompt.dtype)
    new_state_gla = jnp.stack([jnp.stack(p, axis=1) for p in gla_st], axis=1).astype(x_prompt.dtype)
    y_sample, _, _ = trunk(x_sample, c, True, state_ssd, state_gla, *weights)
    return (y_prompt, y_sample, new_state_ssd, new_state_gla)
```

```python
import functools

import jax
import jax.numpy as jnp
from jax import lax
from jax.experimental import pallas as pl
from jax.experimental.pallas import tpu as pltpu

F32 = jnp.float32
BF16 = jnp.bfloat16
I32 = jnp.int32
HI = lax.Precision.HIGHEST

D = 1024
NB_P, SEQ_P = 32, 256
NB_S, SEQ_S = 4, 4096
GRID_W = 64
DEPTH = 4
EPS = 1e-6
NP = NB_P * SEQ_P
NS = NB_S * SEQ_S
T = NP + NS

DI = 2 * D
SSD_P = 64
SSD_H = DI // SSD_P
SSD_G = 4
SSD_N = 128
SSD_L = 128
SSD_XBC = DI + 2 * SSD_G * SSD_N
GLA_H = 4
GLA_DK = D // 2
GLA_DV = D
GLA_HK = GLA_DK // GLA_H
GLA_HV = GLA_DV // GLA_H
GLA_R = 16
GLA_NORM = 16.0
GLA_L = 64
NE = 32
TOPK = 4
DFF = D
ALPHA = 1.702
LIMIT = 7.0
NAS = T * TOPK

TM = 256
BM = 256
NBLK = NAS // BM
NITEM = NBLK + NE
VMEM_LIMIT = 56 * 1024 * 1024


def _cparams(*sem):
    return pltpu.CompilerParams(dimension_semantics=sem, vmem_limit_bytes=VMEM_LIMIT)


def _nt(a, b, precision=None):
    return lax.dot_general(a, b, (((1,), (1,)), ((), ())), precision=precision,
                           preferred_element_type=F32)


def _mm(a, b, precision=None):
    return jnp.dot(a, b, precision=precision, preferred_element_type=F32)


def _sigmoid(x):
    return 1.0 / (1.0 + jnp.exp(-x))


def _silu(x):
    return x * _sigmoid(x)


def _softplus(x):
    return jnp.maximum(x, 0.0) + jnp.log1p(jnp.exp(-jnp.abs(x)))


def _modnorm(x, g, scale, shift):
    y = x * lax.rsqrt(jnp.mean(x * x, axis=-1, keepdims=True) + EPS) * g
    return y * (1.0 + scale) + shift


def _cond_row(t, tm):
    npt = NP // tm
    return jnp.where(t < npt, 0, 1 + (t - npt) // (SEQ_S // tm))


def _seg_conv(acc, w, width, segm1):
    m = acc.shape[0]
    pos = lax.broadcasted_iota(I32, (m, 1), 0) & segm1
    half = width // 2
    out = acc * w[half:half + 1, :]
    for k in range(width):
        s = k - half
        if s == 0:
            continue
        rolled = pltpu.roll(acc, (m - s) % m, 0)
        valid = ((pos + s >= 0) & (pos + s <= segm1)).astype(F32)
        out = out + (rolled * valid) * w[k:k + 1, :]
    return out


def _mod_kernel(c_ref, w_ref, b_ref, o_ref):
    o_ref[0] = _mm(_silu(c_ref[...]), w_ref[0], HI) + b_ref[0]


def _modulation(cond8, w_mod, b_mod):
    nj = 6
    return pl.pallas_call(
        _mod_kernel,
        grid=(DEPTH, nj),
        in_specs=[pl.BlockSpec((8, D), lambda l, j: (0, 0)),
                  pl.BlockSpec((1, D, D), lambda l, j: (l, 0, j)),
                  pl.BlockSpec((1, 1, D), lambda l, j: (l, 0, j))],
        out_specs=pl.BlockSpec((1, 8, D), lambda l, j: (l, 0, j)),
        out_shape=jax.ShapeDtypeStruct((DEPTH, 8, 6 * D), F32),
        compiler_params=_cparams("parallel", "parallel"),
        name="modulation",
    )(cond8, w_mod, b_mod.reshape(DEPTH, 1, 6 * D))


def _router_epilogue(xn, g2, sc2, sh2, rwT, rb, h2_ref, idx_ref, gcol_ref):
    m = xn.shape[0]
    h2 = _modnorm(xn, g2, sc2, sh2)
    h2_ref[...] = h2
    lg = _nt(rwT, h2, HI) + rb
    eidx = lax.broadcasted_iota(I32, (NE, m), 0)
    vals, idxs = [], []
    for _ in range(TOPK):
        mx = jnp.max(lg, axis=0, keepdims=True)
        sel = jnp.min(jnp.where(lg == mx, eidx, NE), axis=0, keepdims=True)
        vals.append(mx)
        idxs.append(sel)
        lg = jnp.where(eidx == sel, -jnp.inf, lg)
    ex = [jnp.exp(v - vals[0]) for v in vals]
    den = ex[0] + ex[1] + ex[2] + ex[3]
    row4 = lax.broadcasted_iota(I32, (TOPK, m), 0)
    idx = jnp.zeros((TOPK, m), I32)
    for k in range(TOPK):
        idx = jnp.where(row4 == k, idxs[k], idx)
    idx_ref[...] = idx
    row = lax.broadcasted_iota(I32, (128, m), 0)
    slab = jnp.zeros((128, m), F32)
    for k in range(TOPK):
        slab = jnp.where(row == k, ex[k] / den, slab)
    gcol_ref[...] = slab.T


_ROUTER_OUT_SHAPES = (jax.ShapeDtypeStruct((T, D), F32),
                      jax.ShapeDtypeStruct((T, D), F32),
                      jax.ShapeDtypeStruct((TOPK, T), I32),
                      jax.ShapeDtypeStruct((T, 128), F32))


def _router_out_specs(tm, off):
    return [pl.BlockSpec((tm, D), lambda t: (t + off, 0)),
            pl.BlockSpec((tm, D), lambda t: (t + off, 0)),
            pl.BlockSpec((TOPK, tm), lambda t: (0, t + off)),
            pl.BlockSpec((tm, 128), lambda t: (t + off, 0))]


def _router_in_specs(tm, off):
    row = lambda t: (_cond_row(t + off, tm), 0, 0)
    full = lambda t: (0, 0)
    return [pl.BlockSpec((1, 1, D), row),
            pl.BlockSpec((1, D), full),
            pl.BlockSpec((1, 1, D), row),
            pl.BlockSpec((1, 1, D), row),
            pl.BlockSpec((NE, D), full),
            pl.BlockSpec((NE, 1), full)]


def _ssd_in_kernel(x_ref, sh_ref, sc_ref, g_ref, wz_ref, wx_ref, wdt_ref, wdtT_ref, cw_ref, cb_ref,
                   dtb_ref, dtbT_ref, z_ref, xbc_ref, dt_ref, dtT_ref):
    t = pl.program_id(0)
    h = _modnorm(x_ref[...], g_ref[...], sc_ref[0], sh_ref[0])
    hb = h.astype(BF16)
    cn = 512
    for j in range(0, DI, cn):
        z_ref[:, j:j + cn] = _mm(hb, wz_ref[:, j:j + cn]).astype(BF16)
    segm1 = jnp.where(t < NP // TM, SEQ_P - 1, GRID_W - 1)
    for j in range(0, SSD_XBC, cn):
        acc = _mm(hb, wx_ref[:, j:j + cn])
        out = _seg_conv(acc, cw_ref[:, j:j + cn], 5, segm1) + cb_ref[:, j:j + cn]
        xbc_ref[:, j:j + cn] = _silu(out).astype(BF16)
    dt_ref[...] = _softplus(_mm(h, wdt_ref[...], HI) + dtb_ref[...])
    dtT_ref[...] = _softplus(_nt(wdtT_ref[...], h, HI) + dtbT_ref[...])


def _ssd_in(x, shift, scale, g, wz, wx, wdt, conv_w, conv_b, dt_bias):
    row = lambda t: (_cond_row(t, TM), 0, 0)
    full = lambda t: (0, 0)
    nh2 = 2 * SSD_H
    return pl.pallas_call(
        _ssd_in_kernel,
        grid=(T // TM,),
        in_specs=[pl.BlockSpec((TM, D), lambda t: (t, 0)),
                  pl.BlockSpec((1, 1, D), row), pl.BlockSpec((1, 1, D), row),
                  pl.BlockSpec((1, D), full),
                  pl.BlockSpec((D, DI), full), pl.BlockSpec((D, SSD_XBC), full),
                  pl.BlockSpec((D, nh2), full), pl.BlockSpec((nh2, D), full),
                  pl.BlockSpec((5, SSD_XBC), full), pl.BlockSpec((1, SSD_XBC), full),
                  pl.BlockSpec((1, nh2), full), pl.BlockSpec((nh2, 1), full)],
        out_specs=[pl.BlockSpec((TM, DI), lambda t: (t, 0)),
                   pl.BlockSpec((TM, SSD_XBC), lambda t: (t, 0)),
                   pl.BlockSpec((TM, nh2), lambda t: (t, 0)),
                   pl.BlockSpec((nh2, TM), lambda t: (0, t))],
        out_shape=[jax.ShapeDtypeStruct((T, DI), BF16), jax.ShapeDtypeStruct((T, SSD_XBC), BF16),
                   jax.ShapeDtypeStruct((T, nh2), F32), jax.ShapeDtypeStruct((nh2, T), F32)],
        compiler_params=_cparams("parallel"),
        name="ssd_in",
    )(x, shift, scale, g, wz, wx, wdt, wdt.T, conv_w, conv_b.reshape(1, -1),
      dt_bias.reshape(1, nh2), dt_bias.reshape(nh2, 1))


def _ssd_scan_kernel(*refs, rev, has_h0, want_hf, nchunk):
    refs = list(refs)
    xbc_ref, dt_ref, dtT_ref, a_ref, aT_ref = refs[:5]
    pos = 5
    h0_ref = None
    if has_h0:
        h0_ref = refs[pos]
        pos += 1
    y_ref = refs[pos]
    pos += 1
    hf_ref = None
    if want_hf:
        hf_ref = refs[pos]
        pos += 1
    s_ref = refs[pos]
    c = pl.program_id(1)
    L = SSD_L
    d0 = SSD_H if rev else 0

    @pl.when(c == 0)
    def _init():
        if has_h0:
            s_ref[...] = h0_ref[0].T
        else:
            s_ref[...] = jnp.zeros_like(s_ref)

    ii = lax.broadcasted_iota(I32, (L, L), 0)
    jj = lax.broadcasted_iota(I32, (L, L), 1)
    mask = (jj >= ii) if rev else (jj <= ii)
    tri = mask.astype(F32)
    dt = dt_ref[:, d0:d0 + SSD_H]
    dtT = dtT_ref[...]
    acum = _mm(tri, dt * a_ref[...], HI)
    acumT = _nt(dtT * aT_ref[...], tri, HI)
    last = 0 if rev else L - 1
    tot = acum[last:last + 1, :]
    expand = ((lax.broadcasted_iota(I32, (SSD_H, DI), 1) >> 6)
              == lax.broadcasted_iota(I32, (SSD_H, DI), 0)).astype(F32)
    eacum_x = _mm(jnp.exp(acum), expand, HI)
    wcol_x = _mm(jnp.exp(tot - acum) * dt, expand, HI)
    etot_x = _mm(jnp.broadcast_to(jnp.exp(tot), (8, SSD_H)), expand, HI)[0:1, :]
    lane = lax.broadcasted_iota(I32, (L, 128), 1)
    for g in range(SSD_G):
        gs = slice(g * 512, (g + 1) * 512)
        bg = xbc_ref[:, DI + g * SSD_N:DI + (g + 1) * SSD_N]
        cg = xbc_ref[:, DI + SSD_G * SSD_N + g * SSD_N:DI + SSD_G * SSD_N + (g + 1) * SSD_N]
        cb = _nt(cg, bg)
        sg = s_ref[:, gs]
        yint = _mm(cg, sg.astype(BF16))
        xg = xbc_ref[:, gs]
        for p in range(4):
            ms = []
            for hh in (g * 8 + 2 * p, g * 8 + 2 * p + 1):
                seg = acum[:, hh:hh + 1] - acumT[hh:hh + 1, :]
                dec = jnp.exp(jnp.where(mask, seg, -1e30))
                ms.append((cb * dec * dtT[hh:hh + 1, :]).astype(BF16))
            lhs = jnp.concatenate(ms, axis=1)
            xp = xg[:, p * 128:(p + 1) * 128]
            rhs = jnp.concatenate([jnp.where(lane < SSD_P, xp, jnp.zeros_like(xp)),
                                   jnp.where(lane >= SSD_P, xp, jnp.zeros_like(xp))], axis=0)
            cs = slice(g * 512 + p * 128, g * 512 + (p + 1) * 128)
            yp = _mm(lhs, rhs) + yint[:, p * 128:(p + 1) * 128] * eacum_x[:, cs]
            y_ref[:, cs] = yp.astype(BF16)
        xw = (xg.astype(F32) * wcol_x[:, gs]).astype(BF16)
        st = _mm(bg.astype(F32).T.astype(BF16), xw)
        s_ref[:, gs] = sg * etot_x[:, gs] + st

    if want_hf:
        @pl.when(c == nchunk - 1)
        def _fin():
            hf_ref[0] = s_ref[...].T


def _ssd_scan(xbc, dt, dtT, a, h0, y_prev, *, rev, nb, nchunk, base, want_hf):
    has_h0 = h0 is not None
    d = 1 if rev else 0
    cbase = base // SSD_L

    def tok(b, c):
        return cbase + b * nchunk + (nchunk - 1 - c if rev else c)

    in_specs = [pl.BlockSpec((SSD_L, SSD_XBC), lambda b, c: (tok(b, c), 0)),
                pl.BlockSpec((SSD_L, 2 * SSD_H), lambda b, c: (tok(b, c), 0)),
                pl.BlockSpec((SSD_H, SSD_L), lambda b, c: (d, tok(b, c))),
                pl.BlockSpec((1, SSD_H), lambda b, c: (0, 0)),
                pl.BlockSpec((SSD_H, 1), lambda b, c: (0, 0))]
    args = [xbc, dt, dtT, a[d].reshape(1, SSD_H), a[d].reshape(SSD_H, 1)]
    if has_h0:
        in_specs.append(pl.BlockSpec((1, DI, SSD_N), lambda b, c: (b, 0, 0)))
        args.append(h0)
    out_specs = [pl.BlockSpec((SSD_L, DI), lambda b, c: (tok(b, c), 0))]
    out_shape = [jax.ShapeDtypeStruct((T, DI), BF16)]
    if want_hf:
        out_specs.append(pl.BlockSpec((1, DI, SSD_N), lambda b, c: (b, 0, 0)))
        out_shape.append(jax.ShapeDtypeStruct((nb, DI, SSD_N), F32))
    aliases = {}
    if y_prev is not None:
        in_specs.append(pl.BlockSpec(memory_space=pl.ANY))
        args.append(y_prev)
        aliases = {len(args) - 1: 0}

    def body(*refs):
        refs = list(refs)
        if y_prev is not None:
            n_in = len(args)
            refs = refs[:n_in - 1] + refs[n_in:]
        _ssd_scan_kernel(*refs, rev=rev, has_h0=has_h0, want_hf=want_hf, nchunk=nchunk)

    return pl.pallas_call(
        body,
        grid=(nb, nchunk),
        in_specs=in_specs, out_specs=out_specs, out_shape=out_shape,
        scratch_shapes=[pltpu.VMEM((SSD_N, DI), F32)],
        input_output_aliases=aliases,
        compiler_params=_cparams("parallel", "arbitrary"),
        name="ssd_scan_" + ("b" if rev else "f") + ("_s" if has_h0 else "_p"),
    )(*args)


def _ssd_out_kernel(yf_ref, yb_ref, x2_ref, z_ref, dsk_ref, ng_ref, wo_ref, xres_ref,
                    gate_ref, g2_ref, sc2_ref, sh2_ref, rwT_ref, rb_ref,
                    xn_ref, h2_ref, idx_ref, gcol_ref):
    y = yf_ref[...].astype(F32) + yb_ref[...].astype(F32) + dsk_ref[...] * x2_ref[...].astype(F32)
    y = y * _silu(z_ref[...].astype(F32))
    y = y * lax.rsqrt(jnp.mean(y * y, axis=-1, keepdims=True) + EPS) * ng_ref[...]
    xn = xres_ref[...] + gate_ref[0] * _mm(y.astype(BF16), wo_ref[...])
    xn_ref[...] = xn
    _router_epilogue(xn, g2_ref[...], sc2_ref[0], sh2_ref[0], rwT_ref[...], rb_ref[...],
                     h2_ref, idx_ref, gcol_ref)


def _ssd_out(yf, yb, xbc, z, dskip_x, norm_g, wo, x, router_args):
    full = lambda t: (0, 0)
    rowb = lambda t: (t, 0)
    return pl.pallas_call(
        _ssd_out_kernel,
        grid=(T // TM,),
        in_specs=[pl.BlockSpec((TM, DI), rowb), pl.BlockSpec((TM, DI), rowb),
                  pl.BlockSpec((TM, DI), rowb), pl.BlockSpec((TM, DI), rowb),
                  pl.BlockSpec((1, DI), full), pl.BlockSpec((1, DI), full),
                  pl.BlockSpec((DI, D), full), pl.BlockSpec((TM, D), rowb)] + _router_in_specs(TM, 0),
        out_specs=_router_out_specs(TM, 0),
        out_shape=_ROUTER_OUT_SHAPES,
        compiler_params=_cparams("parallel"),
        name="ssd_out",
    )(yf, yb, xbc, z, dskip_x, norm_g, wo, x, *router_args)


def _gla_in_kernel(x_ref, sh_ref, sc_ref, g_ref, wq_ref, wk_ref, wv_ref, wr_ref, wg_ref, w2_ref, b2_ref,
                   q_ref, k_ref, v_ref, r_ref, gl_ref, *scratch, colmajor):
    h = _modnorm(x_ref[...], g_ref[...], sc_ref[0], sh_ref[0])
    hb = h.astype(BF16)
    glow = _mm(h, wg_ref[...], HI)
    gls = []
    for zdir in range(2):
        pre = _mm(glow[:, zdir * GLA_R:(zdir + 1) * GLA_R], w2_ref[zdir], HI) + b2_ref[zdir]
        gls.append((jnp.minimum(pre, 0.0) - jnp.log1p(jnp.exp(-jnp.abs(pre)))) / GLA_NORM)
    outs = [(_mm(hb, wq_ref[...]) * (GLA_HK ** -0.5)).astype(BF16),
            _mm(hb, wk_ref[...]).astype(BF16),
            _mm(hb, wv_ref[...]).astype(BF16),
            _mm(hb, wr_ref[...]).astype(BF16),
            jnp.concatenate(gls, axis=1)]
    orefs = [q_ref, k_ref, v_ref, r_ref, gl_ref]
    if not colmajor:
        for o_ref, val in zip(orefs, outs):
            o_ref[...] = val
    else:
        for o_ref, val, scr in zip(orefs, outs, scratch):
            for jc in range(val.shape[1] // 128):
                scr[jc] = val[:, jc * 128:(jc + 1) * 128].astype(F32)
            for col in range(GRID_W):
                for jc in range(val.shape[1] // 128):
                    o_ref[0, col, :, jc * 128:(jc + 1) * 128] = (
                        scr[jc, pl.ds(col, 8, stride=GRID_W), :].astype(o_ref.dtype))


def _gla_in(x, shift, scale, g, wq, wk, wv, wr, wg, w2, b2, *, colmajor):
    tm = 8 * GRID_W if colmajor else TM
    off = NP // tm if colmajor else 0
    ntile = (NS if colmajor else NP) // tm
    row = lambda t: (_cond_row(t + off, tm), 0, 0)
    full = lambda t: (0, 0)
    full3 = lambda t: (0, 0, 0)
    widths = [GLA_DK, GLA_DK, GLA_DV, GLA_DV, 2 * GLA_DK]
    dts = [BF16, BF16, BF16, BF16, F32]
    if colmajor:
        out_specs = [pl.BlockSpec((1, GRID_W, 8, w), lambda t: (t // 8, 0, t % 8, 0)) for w in widths]
        out_shape = [jax.ShapeDtypeStruct((NB_S, GRID_W, GRID_W, w), dt) for w, dt in zip(widths, dts)]
        scratch = [pltpu.VMEM((w // 128, tm, 128), F32) for w in widths]
    else:
        out_specs = [pl.BlockSpec((tm, w), lambda t: (t, 0)) for w in widths]
        out_shape = [jax.ShapeDtypeStruct((NP, w), dt) for w, dt in zip(widths, dts)]
        scratch = []
    return pl.pallas_call(
        functools.partial(_gla_in_kernel, colmajor=colmajor),
        grid=(ntile,),
        in_specs=[pl.BlockSpec((tm, D), lambda t: (t + off, 0)),
                  pl.BlockSpec((1, 1, D), row), pl.BlockSpec((1, 1, D), row),
                  pl.BlockSpec((1, D), full),
                  pl.BlockSpec((D, GLA_DK), full), pl.BlockSpec((D, GLA_DK), full),
                  pl.BlockSpec((D, GLA_DV), full), pl.BlockSpec((D, GLA_DV), full),
                  pl.BlockSpec((D, 2 * GLA_R), full),
                  pl.BlockSpec((2, GLA_R, GLA_DK), full3), pl.BlockSpec((2, 1, GLA_DK), full3)],
        out_specs=out_specs, out_shape=out_shape, scratch_shapes=scratch,
        compiler_params=_cparams("parallel"),
        name="gla_in_" + ("s" if colmajor else "p"),
    )(x, shift, scale, g, wq, wk, wv, wr, wg, w2, b2.reshape(2, 1, GLA_DK))


def _gla_scan_kernel(*refs, rev, has_s0, want_sf, nchunk):
    refs = list(refs)
    q_ref, k_ref, v_ref, gl_ref = refs[:4]
    pos = 4
    s0_ref = None
    if has_s0:
        s0_ref = refs[pos]
        pos += 1
    o_ref = refs[pos]
    pos += 1
    sf_ref = None
    if want_sf:
        sf_ref = refs[pos]
        pos += 1
    st_ref = refs[pos]
    c = pl.program_id(1)
    L = GLA_L

    @pl.when(c == 0)
    def _init():
        for hh in range(GLA_H):
            if has_s0:
                st_ref[hh] = s0_ref[0, hh].T
            else:
                st_ref[hh] = jnp.zeros((GLA_HV, GLA_HK), F32)

    ii = lax.broadcasted_iota(I32, (L, L), 0)
    jj = lax.broadcasted_iota(I32, (L, L), 1)
    mask = (jj >= ii) if rev else (jj <= ii)
    gk = gl_ref[0]
    g = _mm(mask.astype(F32), gk, HI)
    ri = L - 1 - L // 2 if rev else L // 2
    ei = 0 if rev else L - 1
    gref = g[ri:ri + 1, :]
    gend = g[ei:ei + 1, :]
    qf = q_ref[0].astype(F32)
    kf = k_ref[0].astype(F32)
    qg = (qf * jnp.exp(g - gref)).astype(BF16)
    kg = (kf * jnp.exp(gref - g)).astype(BF16)
    qe = (qf * jnp.exp(g)).astype(BF16)
    ku = (kf * jnp.exp(gend - g)).astype(BF16)
    dec = jnp.exp(gend)
    v = v_ref[0]
    for hh in range(GLA_H):
        ks = slice(hh * GLA_HK, (hh + 1) * GLA_HK)
        vs = slice(hh * GLA_HV, (hh + 1) * GLA_HV)
        att = jnp.where(mask, _nt(qg[:, ks], kg[:, ks]), 0.0)
        vh = v[:, vs]
        st = st_ref[hh]
        o = _mm(att.astype(BF16), vh) + _nt(qe[:, ks], st.astype(BF16))
        o_ref[0, :, vs] = o.astype(BF16)
        ut = _mm(vh.astype(F32).T.astype(BF16), ku[:, ks])
        st_ref[hh] = st * dec[:, ks] + ut

    if want_sf:
        @pl.when(c == nchunk - 1)
        def _fin():
            for hh in range(GLA_H):
                sf_ref[0, hh] = st_ref[hh].T


def _gla_scan(q, k, v, gl, s0, *, rev, nb, nchunk, want_sf):
    has_s0 = s0 is not None
    d = 1 if rev else 0
    ch = (lambda c: nchunk - 1 - c) if rev else (lambda c: c)
    blk = lambda w: pl.BlockSpec((1, GLA_L, w), lambda b, c: (b * nchunk + ch(c), 0, 0))
    in_specs = [blk(GLA_DK), blk(GLA_DK), blk(GLA_DV),
                pl.BlockSpec((1, GLA_L, GLA_DK), lambda b, c: (b * nchunk + ch(c), 0, d))]
    args = [t.reshape(nb * nchunk, GLA_L, t.shape[-1]) for t in (q, k, v, gl)]
    if has_s0:
        in_specs.append(pl.BlockSpec((1, GLA_H, GLA_HK, GLA_HV), lambda b, c: (b, 0, 0, 0)))
        args.append(s0)
    out_specs = [blk(GLA_DV)]
    out_shape = [jax.ShapeDtypeStruct((nb * nchunk, GLA_L, GLA_DV), BF16)]
    if want_sf:
        out_specs.append(pl.BlockSpec((1, GLA_H, GLA_HK, GLA_HV), lambda b, c: (b, 0, 0, 0)))
        out_shape.append(jax.ShapeDtypeStruct((nb, GLA_H, GLA_HK, GLA_HV), F32))
    return pl.pallas_call(
        functools.partial(_gla_scan_kernel, rev=rev, has_s0=has_s0, want_sf=want_sf, nchunk=nchunk),
        grid=(nb, nchunk),
        in_specs=in_specs, out_specs=out_specs, out_shape=out_shape,
        scratch_shapes=[pltpu.VMEM((GLA_H, GLA_HV, GLA_HK), F32)],
        compiler_params=_cparams("parallel", "arbitrary"),
        name="gla_scan_" + ("b" if rev else "f") + ("_s" if has_s0 else "_p"),
    )(*args)


def _gla_out_kernel(of_ref, ob_ref, r_ref, ng_ref, wo_ref, xres_ref,
                    gate_ref, g2_ref, sc2_ref, sh2_ref, rwT_ref, rb_ref, *rest, colmajor):
    if colmajor:
        (_, _, _, _, xn_ref, h2_ref, idx_ref, gcol_ref, so_ref, sr_ref) = rest
        nc = GLA_DV // 128
        for col in range(GRID_W):
            osum = of_ref[0, col].astype(F32) + ob_ref[0, col].astype(F32)
            rcol = r_ref[0, col].astype(F32)
            for jc in range(nc):
                so_ref[jc, pl.ds(col, 8, stride=GRID_W), :] = osum[:, jc * 128:(jc + 1) * 128]
                sr_ref[jc, pl.ds(col, 8, stride=GRID_W), :] = rcol[:, jc * 128:(jc + 1) * 128]
        o = jnp.concatenate([so_ref[jc] for jc in range(nc)], axis=1)
        r = jnp.concatenate([sr_ref[jc] for jc in range(nc)], axis=1)
    else:
        xn_ref, h2_ref, idx_ref, gcol_ref = rest
        o = of_ref[...].astype(F32) + ob_ref[...].astype(F32)
        r = r_ref[...].astype(F32)
    parts = []
    for hh in range(GLA_H):
        oh = o[:, hh * GLA_HV:(hh + 1) * GLA_HV]
        parts.append(oh * lax.rsqrt(jnp.mean(oh * oh, axis=-1, keepdims=True) + EPS) * ng_ref[...])
    y = jnp.concatenate(parts, axis=1) * _silu(r)
    xn = xres_ref[...] + gate_ref[0] * _mm(y.astype(BF16), wo_ref[...])
    xn_ref[...] = xn
    _router_epilogue(xn, g2_ref[...], sc2_ref[0], sh2_ref[0], rwT_ref[...], rb_ref[...],
                     h2_ref, idx_ref, gcol_ref)


def _gla_out(of, ob, r, norm_g, wo, x, router_args, prev, *, colmajor):
    tm = 8 * GRID_W if colmajor else TM
    off = NP // tm if colmajor else 0
    ntile = (NS if colmajor else NP) // tm
    full = lambda t: (0, 0)
    if colmajor:
        blk = lambda: pl.BlockSpec((1, GRID_W, 8, GLA_DV), lambda t: (t // 8, 0, t % 8, 0))
        scratch = [pltpu.VMEM((GLA_DV // 128, tm, 128), F32), pltpu.VMEM((GLA_DV // 128, tm, 128), F32)]
    else:
        blk = lambda: pl.BlockSpec((tm, GLA_DV), lambda t: (t, 0))
        scratch = []
    in_specs = ([blk(), blk(), blk(), pl.BlockSpec((1, GLA_HV), full), pl.BlockSpec((GLA_DV, D), full),
                 pl.BlockSpec((tm, D), lambda t: (t + off, 0))] + _router_in_specs(tm, off))
    args = [of, ob, r, norm_g, wo, x, *router_args]
    aliases = {}
    if prev is not None:
        for i, p in enumerate(prev):
            in_specs.append(pl.BlockSpec(memory_space=pl.ANY))
            args.append(p)
            aliases[len(args) - 1] = i
    return pl.pallas_call(
        functools.partial(_gla_out_kernel, colmajor=colmajor),
        grid=(ntile,),
        in_specs=in_specs, out_specs=_router_out_specs(tm, off), out_shape=_ROUTER_OUT_SHAPES,
        scratch_shapes=scratch, input_output_aliases=aliases,
        compiler_params=_cparams("parallel"),
        name="gla_out_" + ("s" if colmajor else "p"),
    )(*args)


def _sc_kernel(x_ref, sh_ref, sc_ref, g_ref, wb_ref, wc_ref, wu_ref, cw_ref, wo_ref,
               gate_ref, g2_ref, sc2_ref, sh2_ref, rwT_ref, rb_ref,
               xn_ref, h2_ref, idx_ref, gcol_ref, y_scr):
    t = pl.program_id(0)
    x = x_ref[...]
    hb = _modnorm(x, g_ref[...], sc_ref[0], sh_ref[0]).astype(BF16)
    segm1 = jnp.where(t < NP // TM, SEQ_P - 1, GRID_W - 1)
    cn = 512
    for j in range(0, D, cn):
        js = slice(j, j + cn)
        gcu = _mm(hb, wc_ref[:, js]) * _mm(hb, wu_ref[:, js])
        y_scr[:, js] = (_mm(hb, wb_ref[:, js]) * _seg_conv(gcu, cw_ref[:, js], 3, segm1)).astype(BF16)
    xn = x + gate_ref[0] * _mm(y_scr[...], wo_ref[...])
    xn_ref[...] = xn
    _router_epilogue(xn, g2_ref[...], sc2_ref[0], sh2_ref[0], rwT_ref[...], rb_ref[...],
                     h2_ref, idx_ref, gcol_ref)


def _shortconv(x, shift, scale, g, wb, wc, wu, conv_w, wo, router_args):
    row = lambda t: (_cond_row(t, TM), 0, 0)
    full = lambda t: (0, 0)
    return pl.pallas_call(
        _sc_kernel,
        grid=(T // TM,),
        in_specs=[pl.BlockSpec((TM, D), lambda t: (t, 0)),
                  pl.BlockSpec((1, 1, D), row), pl.BlockSpec((1, 1, D), row),
                  pl.BlockSpec((1, D), full),
                  pl.BlockSpec((D, D), full), pl.BlockSpec((D, D), full), pl.BlockSpec((D, D), full),
                  pl.BlockSpec((3, D), full), pl.BlockSpec((D, D), full)] + _router_in_specs(TM, 0),
        out_specs=_router_out_specs(TM, 0), out_shape=_ROUTER_OUT_SHAPES,
        scratch_shapes=[pltpu.VMEM((TM, D), BF16)],
        compiler_params=_cparams("parallel"),
        name="shortconv",
    )(x, shift, scale, g, wb, wc, wu, conv_w, wo, *router_args)


def _moe_kernel(ib_ref, ie_ref, ilo_ref, ihi_ref, ifirst_ref, ilast_ref, ichg_ref,
                ord_ref, tok_ref, h_hbm, wgu_ref, bgu_ref, wd_ref, bd_ref, y_hbm,
                xbuf, obuf, wgu_bf, wd_bf, gsem, ssem):
    i = pl.program_id(0)

    @pl.when(ichg_ref[i] == 1)
    def _cast_weights():
        wgu_bf[...] = wgu_ref[0].astype(BF16)
        wd_bf[...] = wd_ref[0].astype(BF16)

    def gather_copy(r):
        return pltpu.make_async_copy(h_hbm.at[pl.ds(tok_ref[0, 0, r], 1), :],
                                     xbuf.at[pl.ds(r, 1), :], gsem)

    def scatter_copy(r):
        return pltpu.make_async_copy(obuf.at[pl.ds(r, 1), :],
                                     y_hbm.at[pl.ds(ord_ref[0, 0, r], 1), :], ssem)

    @pl.when(ifirst_ref[i] == 1)
    def _gather():
        lax.fori_loop(0, BM, lambda r, c: (gather_copy(r).start(), c)[1], 0)
        lax.fori_loop(0, BM, lambda r, c: (gather_copy(r).wait(), c)[1], 0)
        obuf[...] = jnp.zeros_like(obuf)

    lo = ilo_ref[i]
    hi = ihi_ref[i]

    @pl.when(hi > lo)
    def _compute():
        xb = xbuf[...].astype(BF16)
        gu = _mm(xb, wgu_bf[...]) + bgu_ref[0]
        gg = jnp.minimum(gu[:, :DFF], LIMIT)
        uu = jnp.clip(gu[:, DFF:], -LIMIT, LIMIT)
        act = gg * _sigmoid(ALPHA * gg) * (uu + 1.0)
        yo = _mm(act.astype(BF16), wd_bf[...]) + bd_ref[0]
        rows = lax.broadcasted_iota(I32, (BM, 1), 0)
        keep = (rows >= lo) & (rows < hi)
        obuf[...] = jnp.where(keep, yo, obuf[...])

    @pl.when(ilast_ref[i] == 1)
    def _scatter():
        lax.fori_loop(0, BM, lambda r, c: (scatter_copy(r).start(), c)[1], 0)
        lax.fori_loop(0, BM, lambda r, c: (scatter_copy(r).wait(), c)[1], 0)


def _moe_experts(items, order3, tok3, h2, w_gu, b_gu, w_down, b_down):
    nsp = 7
    grid_spec = pltpu.PrefetchScalarGridSpec(
        num_scalar_prefetch=nsp,
        grid=(NITEM,),
        in_specs=[pl.BlockSpec((1, 1, BM), lambda i, ib, ie, *_: (ib[i], 0, 0), memory_space=pltpu.SMEM),
                  pl.BlockSpec((1, 1, BM), lambda i, ib, ie, *_: (ib[i], 0, 0), memory_space=pltpu.SMEM),
                  pl.BlockSpec(memory_space=pl.ANY),
                  pl.BlockSpec((1, D, 2 * DFF), lambda i, ib, ie, *_: (ie[i], 0, 0)),
                  pl.BlockSpec((1, 1, 2 * DFF), lambda i, ib, ie, *_: (ie[i], 0, 0)),
                  pl.BlockSpec((1, DFF, D), lambda i, ib, ie, *_: (ie[i], 0, 0)),
                  pl.BlockSpec((1, 1, D), lambda i, ib, ie, *_: (ie[i], 0, 0))],
        out_specs=pl.BlockSpec(memory_space=pl.ANY),
        scratch_shapes=[pltpu.VMEM((BM, D), F32), pltpu.VMEM((BM, D), F32),
                        pltpu.VMEM((D, 2 * DFF), BF16), pltpu.VMEM((DFF, D), BF16),
                        pltpu.SemaphoreType.DMA(()), pltpu.SemaphoreType.DMA(())],
    )
    return pl.pallas_call(
        _moe_kernel,
        grid_spec=grid_spec,
        out_shape=jax.ShapeDtypeStruct((NAS, D), F32),
        compiler_params=_cparams("arbitrary"),
        name="moe_experts",
    )(*items, order3, tok3, h2, w_gu, b_gu.reshape(NE, 1, 2 * DFF), w_down, b_down.reshape(NE, 1, D))


def _moe_plan(idx_t):
    e_flat = idx_t.reshape(NAS)
    order = jnp.argsort(e_flat).astype(I32)
    counts = jnp.zeros((NE,), I32).at[e_flat].add(1)
    off = jnp.concatenate([jnp.zeros((1,), I32), jnp.cumsum(counts).astype(I32)])
    fb = off[:-1] // BM
    lb = (off[1:] - 1) // BM
    n_e = jnp.where(counts > 0, lb - fb + 1, 0)
    istart = jnp.cumsum(n_e) - n_e
    total = jnp.sum(n_e)
    ids = jnp.arange(NITEM, dtype=I32)
    real = ids < total
    ie = jnp.clip(jnp.searchsorted(istart + n_e, ids, side='right'), 0, NE - 1).astype(I32)
    ib = fb[ie] + ids - istart[ie]
    last_e = ie[jnp.maximum(total - 1, 0)]
    ie = jnp.where(real, ie, last_e)
    ib = jnp.where(real, ib, NBLK - 1).astype(I32)
    lo = jnp.where(real, jnp.maximum(off[ie], ib * BM) - ib * BM, 0).astype(I32)
    hi = jnp.where(real, jnp.minimum(off[ie + 1], (ib + 1) * BM) - ib * BM, 0).astype(I32)
    prev_b = jnp.concatenate([jnp.full((1,), -1, I32), ib[:-1]])
    next_b = jnp.concatenate([ib[1:], jnp.full((1,), -1, I32)])
    prev_e = jnp.concatenate([jnp.full((1,), -1, I32), ie[:-1]])
    first = (real & (ib != prev_b)).astype(I32)
    last = (real & ((ib != next_b) | (ids == total - 1))).astype(I32)
    chg = (ie != prev_e).astype(I32)
    tok = order % T
    return (ib, ie, lo, hi, first, last, chg), order.reshape(NBLK, 1, BM), tok.reshape(NBLK, 1, BM)


def _combine_kernel(x_ref, y4_ref, gcol_ref, gate_ref, *rest, final):
    if final:
        fg_ref, o_ref = rest
    else:
        (o_ref,) = rest
    gc = gcol_ref[...]
    acc = y4_ref[0] * gc[:, 0:1]
    for k in range(1, TOPK):
        acc = acc + y4_ref[k] * gc[:, k:k + 1]
    xn = x_ref[...] + gate_ref[0] * acc
    if final:
        xn = xn * lax.rsqrt(jnp.mean(xn * xn, axis=-1, keepdims=True) + EPS) * fg_ref[...]
    o_ref[...] = xn


def _combine(x, y4, gcol, gate2, final_g):
    final = final_g is not None
    in_specs = [pl.BlockSpec((TM, D), lambda t: (t, 0)),
                pl.BlockSpec((TOPK, TM, D), lambda t: (0, t, 0)),
                pl.BlockSpec((TM, 128), lambda t: (t, 0)),
                pl.BlockSpec((1, 1, D), lambda t: (_cond_row(t, TM), 0, 0))]
    args = [x, y4.reshape(TOPK, T, D), gcol, gate2]
    if final:
        in_specs.append(pl.BlockSpec((1, D), lambda t: (0, 0)))
        args.append(final_g)
    return pl.pallas_call(
        functools.partial(_combine_kernel, final=final),
        grid=(T // TM,),
        in_specs=in_specs,
        out_specs=pl.BlockSpec((TM, D), lambda t: (t, 0)),
        out_shape=jax.ShapeDtypeStruct((T, D), F32),
        compiler_params=_cparams("parallel"),
        name="moe_combine",
    )(*args)


def kernel(x_prompt, x_sample, state_ssd, state_gla, c, c_ctx, w_mod, b_mod, norm1_g, norm2_g, ssd_w_in, ssd_conv_w, ssd_conv_b, ssd_dt_bias, ssd_a_log, ssd_d, ssd_norm_g, ssd_w_out, gla_w_in, gla_w_gate2, gla_b_gate2, gla_norm_g, gla_w_out, sc_w_in, sc_conv_w, sc_w_out, router_w, router_b, moe_w_gu, moe_b_gu, moe_w_down, moe_b_down, final_norm_g):
    x = jnp.concatenate([x_prompt.reshape(NP, D), x_sample.reshape(NS, D)], axis=0)
    cond8 = jnp.concatenate([c_ctx[None, :], c, jnp.zeros((8 - 1 - NB_S, D), F32)], axis=0)
    mod = _modulation(cond8, w_mod, b_mod)

    new_ssd, new_gla = [], []
    for i in range(DEPTH):
        m6 = [mod[i, :, j * D:(j + 1) * D].reshape(8, 1, D) for j in range(6)]
        shift1, scale1, gate1, shift2, scale2, gate2 = m6
        n1 = norm1_g[i].reshape(1, D)
        router_args = (gate1, norm2_g[i].reshape(1, D), scale2, shift2,
                       router_w[i].T, router_b[i].reshape(NE, 1))
        kind, j = i % 3, i // 3
        if kind == 0:
            w_in = ssd_w_in[j]
            wz = w_in[:, :DI].astype(BF16)
            wx = w_in[:, DI:DI + SSD_XBC].astype(BF16)
            wdt = w_in[:, DI + SSD_XBC:]
            z, xbc, dt, dtT = _ssd_in(x, shift1, scale1, n1, wz, wx, wdt, ssd_conv_w[j], ssd_conv_b[j],
                                      ssd_dt_bias[j].reshape(-1))
            a = -jnp.exp(ssd_a_log[j].astype(F32))
            ys, hfs = [], []
            for rev in (False, True):
                d = 1 if rev else 0
                yp, hf = _ssd_scan(xbc, dt, dtT, a, None, None, rev=rev, nb=NB_P,
                                   nchunk=SEQ_P // SSD_L, base=0, want_hf=True)
                h0 = state_ssd[:, j, d].reshape(NB_S, DI, SSD_N)
                (yd,) = _ssd_scan(xbc, dt, dtT, a, h0, yp, rev=rev, nb=NB_S,
                                  nchunk=SEQ_S // SSD_L, base=NP, want_hf=False)
                ys.append(yd)
                hfs.append(hf.reshape(NB_P, SSD_H, SSD_P, SSD_N))
            new_ssd.append(jnp.stack(hfs, axis=1))
            dskip_x = jnp.repeat(ssd_d[j].astype(F32), SSD_P).reshape(1, DI)
            x, h2, idx_t, gcol = _ssd_out(ys[0], ys[1], xbc, z, dskip_x, ssd_norm_g[j].reshape(1, DI),
                                          ssd_w_out[j].astype(BF16), x, router_args)
        elif kind == 1:
            w_in = gla_w_in[j]
            wq = w_in[:, :GLA_DK].astype(BF16)
            wk = w_in[:, GLA_DK:2 * GLA_DK].astype(BF16)
            wv = w_in[:, 2 * GLA_DK:2 * GLA_DK + GLA_DV].astype(BF16)
            wr = w_in[:, 2 * GLA_DK + GLA_DV:2 * GLA_DK + 2 * GLA_DV].astype(BF16)
            wg = w_in[:, 2 * GLA_DK + 2 * GLA_DV:]
            wo = gla_w_out[j].astype(BF16)
            ng = gla_norm_g[j].reshape(1, GLA_HV)
            prev = None
            sfs = []
            for colmajor in (False, True):
                q, k, v, r, gl = _gla_in(x, shift1, scale1, n1, wq, wk, wv, wr, wg,
                                         gla_w_gate2[j], gla_b_gate2[j], colmajor=colmajor)
                nb = NB_S if colmajor else NB_P
                nchunk = (SEQ_S if colmajor else SEQ_P) // GLA_L
                os_ = []
                for rev in (False, True):
                    d = 1 if rev else 0
                    s0 = state_gla[:, j, d] if colmajor else None
                    res = _gla_scan(q, k, v, gl, s0, rev=rev, nb=nb, nchunk=nchunk, want_sf=not colmajor)
                    o = res[0]
                    if colmajor:
                        o = o.reshape(NB_S, GRID_W, GRID_W, GLA_DV)
                    else:
                        o = o.reshape(NP, GLA_DV)
                        sfs.append(res[1])
                    os_.append(o)
                prev = _gla_out(os_[0], os_[1], r, ng, wo, x, router_args, prev, colmajor=colmajor)
            new_gla.append(jnp.stack(sfs, axis=1))
            x, h2, idx_t, gcol = prev
        else:
            w_in = sc_w_in[j]
            x, h2, idx_t, gcol = _shortconv(x, shift1, scale1, n1, w_in[:, :D].astype(BF16),
                                            w_in[:, D:2 * D].astype(BF16), w_in[:, 2 * D:].astype(BF16),
                                            sc_conv_w[j], sc_w_out[j].astype(BF16), router_args)
        items, order3, tok3 = _moe_plan(idx_t)
        y4 = _moe_experts(items, order3, tok3, h2, moe_w_gu[i], moe_b_gu[i], moe_w_down[i], moe_b_down[i])
        x = _combine(x, y4, gcol, gate2, final_norm_g.reshape(1, D) if i == DEPTH - 1 else None)

    y_prompt = x[:NP].reshape(NB_P, SEQ_P, D)
    y_sample = x[NP:].reshape(NB_S, SEQ_S, D)
    new_state_ssd = jnp.stack(new_ssd, axis=1)
    new_state_gla = jnp.stack(new_gla, axis=1)
    return (y_prompt, y_sample, new_state_ssd, new_state_gla)
```

```python
import functools

import jax
import jax.numpy as jnp
from jax import lax
from jax.experimental import pallas as pl
from jax.experimental.pallas import tpu as pltpu

F32 = jnp.float32
BF16 = jnp.bfloat16
I32 = jnp.int32
HI = lax.Precision.HIGHEST

D = 1024
NB_P, SEQ_P = 32, 256
NB_S, SEQ_S = 4, 4096
GRID_W = 64
DEPTH = 4
EPS = 1e-6
NP = NB_P * SEQ_P
NS = NB_S * SEQ_S
T = NP + NS

DI = 2 * D
SSD_P = 64
SSD_H = DI // SSD_P
SSD_G = 4
SSD_N = 128
SSD_L = 128
SSD_XBC = DI + 2 * SSD_G * SSD_N
GLA_H = 4
GLA_DK = D // 2
GLA_DV = D
GLA_HK = GLA_DK // GLA_H
GLA_HV = GLA_DV // GLA_H
GLA_R = 16
GLA_NORM = 16.0
GLA_L = 64
NE = 32
TOPK = 4
DFF = D
ALPHA = 1.702
LIMIT = 7.0
NAS = T * TOPK

TM = 256
BM = 256
NB_TOT = NAS // BM + NE
Y_ROWS = NAS + 2 * BM
SUB = 8
VMEM_LIMIT = 56 * 1024 * 1024


def _cparams(*sem):
    return pltpu.CompilerParams(dimension_semantics=sem, vmem_limit_bytes=VMEM_LIMIT)


def _nt(a, b, precision=None):
    return lax.dot_general(a, b, (((1,), (1,)), ((), ())), precision=precision,
                           preferred_element_type=F32)


def _mm(a, b, precision=None):
    return jnp.dot(a, b, precision=precision, preferred_element_type=F32)


def _split3(v):
    hi = v.astype(BF16)
    r1 = v - hi.astype(F32)
    mid = r1.astype(BF16)
    lo = (r1 - mid.astype(F32)).astype(BF16)
    return jnp.concatenate([hi, mid, lo], axis=1)


def _sigmoid(x):
    return 1.0 / (1.0 + jnp.exp(-x))


def _silu(x):
    return x * _sigmoid(x)


def _softplus(x):
    return jnp.maximum(x, 0.0) + jnp.log1p(jnp.exp(-jnp.abs(x)))


def _modnorm(x, g, scale, shift):
    y = x * lax.rsqrt(jnp.mean(x * x, axis=-1, keepdims=True) + EPS) * g
    return y * (1.0 + scale) + shift


def _cond_row(t, tm):
    npt = NP // tm
    return jnp.where(t < npt, 0, 1 + (t - npt) // (SEQ_S // tm))


def _seg_conv(acc, w, width, segm1):
    m = acc.shape[0]
    pos = lax.broadcasted_iota(I32, (m, 1), 0) & segm1
    half = width // 2
    out = acc * w[half:half + 1, :]
    for k in range(width):
        s = k - half
        if s == 0:
            continue
        rolled = pltpu.roll(acc, (m - s) % m, 0)
        valid = ((pos + s >= 0) & (pos + s <= segm1)).astype(F32)
        out = out + (rolled * valid) * w[k:k + 1, :]
    return out


def _mod_kernel(c_ref, w_ref, b_ref, o_ref):
    o_ref[0] = _mm(_silu(c_ref[...]), w_ref[0], HI) + b_ref[0]


def _modulation(cond8, w_mod, b_mod):
    nj = 6
    return pl.pallas_call(
        _mod_kernel,
        grid=(DEPTH, nj),
        in_specs=[pl.BlockSpec((8, D), lambda l, j: (0, 0)),
                  pl.BlockSpec((1, D, D), lambda l, j: (l, 0, j)),
                  pl.BlockSpec((1, 1, D), lambda l, j: (l, 0, j))],
        out_specs=pl.BlockSpec((1, 8, D), lambda l, j: (l, 0, j)),
        out_shape=jax.ShapeDtypeStruct((DEPTH, 8, 6 * D), F32),
        compiler_params=_cparams("parallel", "parallel"),
        name="modulation",
    )(cond8, w_mod, b_mod.reshape(DEPTH, 1, 6 * D))


def _router_epilogue(xn, g2, sc2, sh2, rwT, rb, h2_ref, idx_ref, gcol_ref):
    m = xn.shape[0]
    h2 = _modnorm(xn, g2, sc2, sh2)
    _store_tiles(h2_ref, h2, m)
    lg = _nt(rwT, h2, HI) + rb
    eidx = lax.broadcasted_iota(I32, (NE, m), 0)
    vals, idxs = [], []
    for _ in range(TOPK):
        mx = jnp.max(lg, axis=0, keepdims=True)
        sel = jnp.min(jnp.where(lg == mx, eidx, NE), axis=0, keepdims=True)
        vals.append(mx)
        idxs.append(sel)
        lg = jnp.where(eidx == sel, -jnp.inf, lg)
    ex = [jnp.exp(v - vals[0]) for v in vals]
    den = ex[0] + ex[1] + ex[2] + ex[3]
    row4 = lax.broadcasted_iota(I32, (TOPK, m), 0)
    idx = jnp.zeros((TOPK, m), I32)
    for k in range(TOPK):
        idx = jnp.where(row4 == k, idxs[k], idx)
    idx_ref[...] = idx
    row = lax.broadcasted_iota(I32, (128, m), 0)
    slab = jnp.zeros((128, m), F32)
    for k in range(TOPK):
        slab = jnp.where(row == k, ex[k] / den, slab)
    gcol_ref[...] = slab.T


_ROUTER_OUT_SHAPES = (jax.ShapeDtypeStruct((T, D), F32),
                      jax.ShapeDtypeStruct((T * SUB, 128), F32),
                      jax.ShapeDtypeStruct((TOPK, T), I32),
                      jax.ShapeDtypeStruct((T, 128), F32))


def _router_out_specs(tm, off):
    return [pl.BlockSpec((tm, D), lambda t: (t + off, 0)),
            pl.BlockSpec((tm * SUB, 128), lambda t: (t + off, 0)),
            pl.BlockSpec((TOPK, tm), lambda t: (0, t + off)),
            pl.BlockSpec((tm, 128), lambda t: (t + off, 0))]


def _router_in_specs(tm, off):
    row = lambda t: (_cond_row(t + off, tm), 0, 0)
    full = lambda t: (0, 0)
    return [pl.BlockSpec((1, 1, D), row),
            pl.BlockSpec((1, D), full),
            pl.BlockSpec((1, 1, D), row),
            pl.BlockSpec((1, 1, D), row),
            pl.BlockSpec((NE, D), full),
            pl.BlockSpec((NE, 1), full)]


def _ssd_in_kernel(x_ref, sh_ref, sc_ref, g_ref, wz_ref, wx_ref, wdt_ref, wdtT_ref, cw_ref, cb_ref,
                   dtb_ref, dtbT_ref, z_ref, xbc_ref, dt_ref, dtT_ref):
    t = pl.program_id(0)
    h = _modnorm(x_ref[...], g_ref[...], sc_ref[0], sh_ref[0])
    hb = h.astype(BF16)
    cn = 512
    for j in range(0, DI, cn):
        z_ref[:, j:j + cn] = _mm(hb, wz_ref[:, j:j + cn]).astype(BF16)
    segm1 = jnp.where(t < NP // TM, SEQ_P - 1, GRID_W - 1)
    for j in range(0, SSD_XBC, cn):
        acc = _mm(hb, wx_ref[:, j:j + cn])
        out = _seg_conv(acc, cw_ref[:, j:j + cn], 5, segm1) + cb_ref[:, j:j + cn]
        xbc_ref[:, j:j + cn] = _silu(out).astype(BF16)
    dt_ref[...] = _softplus(_mm(h, wdt_ref[...], HI) + dtb_ref[...])
    dtT_ref[...] = _softplus(_nt(wdtT_ref[...], h, HI) + dtbT_ref[...])


def _ssd_in(x, shift, scale, g, wz, wx, wdt, conv_w, conv_b, dt_bias):
    row = lambda t: (_cond_row(t, TM), 0, 0)
    full = lambda t: (0, 0)
    nh2 = 2 * SSD_H
    return pl.pallas_call(
        _ssd_in_kernel,
        grid=(T // TM,),
        in_specs=[pl.BlockSpec((TM, D), lambda t: (t, 0)),
                  pl.BlockSpec((1, 1, D), row), pl.BlockSpec((1, 1, D), row),
                  pl.BlockSpec((1, D), full),
                  pl.BlockSpec((D, DI), full), pl.BlockSpec((D, SSD_XBC), full),
                  pl.BlockSpec((D, nh2), full), pl.BlockSpec((nh2, D), full),
                  pl.BlockSpec((5, SSD_XBC), full), pl.BlockSpec((1, SSD_XBC), full),
                  pl.BlockSpec((1, nh2), full), pl.BlockSpec((nh2, 1), full)],
        out_specs=[pl.BlockSpec((TM, DI), lambda t: (t, 0)),
                   pl.BlockSpec((TM, SSD_XBC), lambda t: (t, 0)),
                   pl.BlockSpec((TM, nh2), lambda t: (t, 0)),
                   pl.BlockSpec((nh2, TM), lambda t: (0, t))],
        out_shape=[jax.ShapeDtypeStruct((T, DI), BF16), jax.ShapeDtypeStruct((T, SSD_XBC), BF16),
                   jax.ShapeDtypeStruct((T, nh2), F32), jax.ShapeDtypeStruct((nh2, T), F32)],
        compiler_params=_cparams("parallel"),
        name="ssd_in",
    )(x, shift, scale, g, wz, wx, wdt, wdt.T, conv_w, conv_b.reshape(1, -1),
      dt_bias.reshape(1, nh2), dt_bias.reshape(nh2, 1))


def _ssd_scan_kernel(*refs, rev, has_h0, want_hf, nchunk):
    refs = list(refs)
    xbc_ref, dt_ref, dtT_ref, a_ref, aT_ref = refs[:5]
    pos = 5
    h0_ref = None
    if has_h0:
        h0_ref = refs[pos]
        pos += 1
    y_ref = refs[pos]
    pos += 1
    hf_ref = None
    if want_hf:
        hf_ref = refs[pos]
        pos += 1
    s_ref = refs[pos]
    c = pl.program_id(1)
    L = SSD_L
    d0 = SSD_H if rev else 0

    @pl.when(c == 0)
    def _init():
        if has_h0:
            s_ref[...] = h0_ref[0].T
        else:
            s_ref[...] = jnp.zeros_like(s_ref)

    ii = lax.broadcasted_iota(I32, (L, L), 0)
    jj = lax.broadcasted_iota(I32, (L, L), 1)
    mask = (jj >= ii) if rev else (jj <= ii)
    tri = mask.astype(F32)
    dt = dt_ref[:, d0:d0 + SSD_H]
    dtT = dtT_ref[...]
    acum = _mm(tri, dt * a_ref[...], HI)
    acumT = _nt(dtT * aT_ref[...], tri, HI)
    last = 0 if rev else L - 1
    tot = acum[last:last + 1, :]
    expand = ((lax.broadcasted_iota(I32, (3 * SSD_H, DI), 1) >> 6)
              == (lax.broadcasted_iota(I32, (3 * SSD_H, DI), 0) & (SSD_H - 1))).astype(BF16)
    eacum_x = _mm(_split3(jnp.exp(acum)), expand)
    wcol_x = _mm(_split3(jnp.exp(tot - acum) * dt), expand)
    etot_x = _mm(_split3(jnp.broadcast_to(jnp.exp(tot), (8, SSD_H))), expand)[0:1, :]
    lane = lax.broadcasted_iota(I32, (L, 128), 1)
    for g in range(SSD_G):
        gs = slice(g * 512, (g + 1) * 512)
        bg = xbc_ref[:, DI + g * SSD_N:DI + (g + 1) * SSD_N]
        cg = xbc_ref[:, DI + SSD_G * SSD_N + g * SSD_N:DI + SSD_G * SSD_N + (g + 1) * SSD_N]
        cb = _nt(cg, bg)
        sg = s_ref[:, gs]
        yint = _mm(cg, sg.astype(BF16))
        xg = xbc_ref[:, gs]
        for p in range(4):
            ms = []
            for hh in (g * 8 + 2 * p, g * 8 + 2 * p + 1):
                seg = acum[:, hh:hh + 1] - acumT[hh:hh + 1, :]
                dec = jnp.exp(jnp.where(mask, seg, -1e30))
                ms.append((cb * dec * dtT[hh:hh + 1, :]).astype(BF16))
            lhs = jnp.concatenate(ms, axis=1)
            xp = xg[:, p * 128:(p + 1) * 128]
            rhs = jnp.concatenate([jnp.where(lane < SSD_P, xp, jnp.zeros_like(xp)),
                                   jnp.where(lane >= SSD_P, xp, jnp.zeros_like(xp))], axis=0)
            cs = slice(g * 512 + p * 128, g * 512 + (p + 1) * 128)
            yp = _mm(lhs, rhs) + yint[:, p * 128:(p + 1) * 128] * eacum_x[:, cs]
            y_ref[:, cs] = yp.astype(BF16)
        xw = (xg.astype(F32) * wcol_x[:, gs]).astype(BF16)
        st = _mm(bg.astype(F32).T.astype(BF16), xw)
        s_ref[:, gs] = sg * etot_x[:, gs] + st

    if want_hf:
        @pl.when(c == nchunk - 1)
        def _fin():
            hf_ref[0] = s_ref[...].T


def _ssd_scan(xbc, dt, dtT, a, h0, y_prev, *, rev, nb, nchunk, base, want_hf):
    has_h0 = h0 is not None
    d = 1 if rev else 0
    cbase = base // SSD_L

    def tok(b, c):
        return cbase + b * nchunk + (nchunk - 1 - c if rev else c)

    in_specs = [pl.BlockSpec((SSD_L, SSD_XBC), lambda b, c: (tok(b, c), 0)),
                pl.BlockSpec((SSD_L, 2 * SSD_H), lambda b, c: (tok(b, c), 0)),
                pl.BlockSpec((SSD_H, SSD_L), lambda b, c: (d, tok(b, c))),
                pl.BlockSpec((1, SSD_H), lambda b, c: (0, 0)),
                pl.BlockSpec((SSD_H, 1), lambda b, c: (0, 0))]
    args = [xbc, dt, dtT, a[d].reshape(1, SSD_H), a[d].reshape(SSD_H, 1)]
    if has_h0:
        in_specs.append(pl.BlockSpec((1, DI, SSD_N), lambda b, c: (b, 0, 0)))
        args.append(h0)
    out_specs = [pl.BlockSpec((SSD_L, DI), lambda b, c: (tok(b, c), 0))]
    out_shape = [jax.ShapeDtypeStruct((T, DI), BF16)]
    if want_hf:
        out_specs.append(pl.BlockSpec((1, DI, SSD_N), lambda b, c: (b, 0, 0)))
        out_shape.append(jax.ShapeDtypeStruct((nb, DI, SSD_N), F32))
    aliases = {}
    if y_prev is not None:
        in_specs.append(pl.BlockSpec(memory_space=pl.ANY))
        args.append(y_prev)
        aliases = {len(args) - 1: 0}

    def body(*refs):
        refs = list(refs)
        if y_prev is not None:
            n_in = len(args)
            refs = refs[:n_in - 1] + refs[n_in:]
        _ssd_scan_kernel(*refs, rev=rev, has_h0=has_h0, want_hf=want_hf, nchunk=nchunk)

    return pl.pallas_call(
        body,
        grid=(nb, nchunk),
        in_specs=in_specs, out_specs=out_specs, out_shape=out_shape,
        scratch_shapes=[pltpu.VMEM((SSD_N, DI), F32)],
        input_output_aliases=aliases,
        compiler_params=_cparams("parallel", "arbitrary"),
        name="ssd_scan_" + ("b" if rev else "f") + ("_s" if has_h0 else "_p"),
    )(*args)


def _ssd_out_kernel(yf_ref, yb_ref, x2_ref, z_ref, dsk_ref, ng_ref, wo_ref, xres_ref,
                    gate_ref, g2_ref, sc2_ref, sh2_ref, rwT_ref, rb_ref,
                    xn_ref, h2_ref, idx_ref, gcol_ref):
    y = yf_ref[...].astype(F32) + yb_ref[...].astype(F32) + dsk_ref[...] * x2_ref[...].astype(F32)
    y = y * _silu(z_ref[...].astype(F32))
    y = y * lax.rsqrt(jnp.mean(y * y, axis=-1, keepdims=True) + EPS) * ng_ref[...]
    xn = xres_ref[...] + gate_ref[0] * _mm(y.astype(BF16), wo_ref[...])
    xn_ref[...] = xn
    _router_epilogue(xn, g2_ref[...], sc2_ref[0], sh2_ref[0], rwT_ref[...], rb_ref[...],
                     h2_ref, idx_ref, gcol_ref)


def _ssd_out(yf, yb, xbc, z, dskip_x, norm_g, wo, x, router_args):
    full = lambda t: (0, 0)
    rowb = lambda t: (t, 0)
    return pl.pallas_call(
        _ssd_out_kernel,
        grid=(T // TM,),
        in_specs=[pl.BlockSpec((TM, DI), rowb), pl.BlockSpec((TM, DI), rowb),
                  pl.BlockSpec((TM, DI), rowb), pl.BlockSpec((TM, DI), rowb),
                  pl.BlockSpec((1, DI), full), pl.BlockSpec((1, DI), full),
                  pl.BlockSpec((DI, D), full), pl.BlockSpec((TM, D), rowb)] + _router_in_specs(TM, 0),
        out_specs=_router_out_specs(TM, 0),
        out_shape=_ROUTER_OUT_SHAPES,
        compiler_params=_cparams("parallel"),
        name="ssd_out",
    )(yf, yb, xbc, z, dskip_x, norm_g, wo, x, *router_args)


def _gla_in_kernel(x_ref, sh_ref, sc_ref, g_ref, wq_ref, wk_ref, wv_ref, wr_ref, wg_ref, w2_ref, b2_ref,
                   q_ref, k_ref, v_ref, r_ref, gl_ref, *scratch, colmajor):
    h = _modnorm(x_ref[...], g_ref[...], sc_ref[0], sh_ref[0])
    hb = h.astype(BF16)
    glow = _mm(h, wg_ref[...], HI)
    gls = []
    for zdir in range(2):
        pre = _mm(glow[:, zdir * GLA_R:(zdir + 1) * GLA_R], w2_ref[zdir], HI) + b2_ref[zdir]
        gls.append((jnp.minimum(pre, 0.0) - jnp.log1p(jnp.exp(-jnp.abs(pre)))) / GLA_NORM)
    outs = [(_mm(hb, wq_ref[...]) * (GLA_HK ** -0.5)).astype(BF16),
            _mm(hb, wk_ref[...]).astype(BF16),
            _mm(hb, wv_ref[...]).astype(BF16),
            _mm(hb, wr_ref[...]).astype(BF16),
            jnp.concatenate(gls, axis=1)]
    orefs = [q_ref, k_ref, v_ref, r_ref, gl_ref]
    if not colmajor:
        for o_ref, val in zip(orefs, outs):
            o_ref[...] = val
    else:
        for o_ref, val, scr in zip(orefs, outs, scratch):
            for jc in range(val.shape[1] // 128):
                scr[jc] = val[:, jc * 128:(jc + 1) * 128].astype(F32)
            for col in range(GRID_W):
                for jc in range(val.shape[1] // 128):
                    o_ref[0, col, :, jc * 128:(jc + 1) * 128] = (
                        scr[jc, pl.ds(col, 8, stride=GRID_W), :].astype(o_ref.dtype))


def _gla_in(x, shift, scale, g, wq, wk, wv, wr, wg, w2, b2, *, colmajor):
    tm = 8 * GRID_W if colmajor else TM
    off = NP // tm if colmajor else 0
    ntile = (NS if colmajor else NP) // tm
    row = lambda t: (_cond_row(t + off, tm), 0, 0)
    full = lambda t: (0, 0)
    full3 = lambda t: (0, 0, 0)
    widths = [GLA_DK, GLA_DK, GLA_DV, GLA_DV, 2 * GLA_DK]
    dts = [BF16, BF16, BF16, BF16, F32]
    if colmajor:
        out_specs = [pl.BlockSpec((1, GRID_W, 8, w), lambda t: (t // 8, 0, t % 8, 0)) for w in widths]
        out_shape = [jax.ShapeDtypeStruct((NB_S, GRID_W, GRID_W, w), dt) for w, dt in zip(widths, dts)]
        scratch = [pltpu.VMEM((w // 128, tm, 128), F32) for w in widths]
    else:
        out_specs = [pl.BlockSpec((tm, w), lambda t: (t, 0)) for w in widths]
        out_shape = [jax.ShapeDtypeStruct((NP, w), dt) for w, dt in zip(widths, dts)]
        scratch = []
    return pl.pallas_call(
        functools.partial(_gla_in_kernel, colmajor=colmajor),
        grid=(ntile,),
        in_specs=[pl.BlockSpec((tm, D), lambda t: (t + off, 0)),
                  pl.BlockSpec((1, 1, D), row), pl.BlockSpec((1, 1, D), row),
                  pl.BlockSpec((1, D), full),
                  pl.BlockSpec((D, GLA_DK), full), pl.BlockSpec((D, GLA_DK), full),
                  pl.BlockSpec((D, GLA_DV), full), pl.BlockSpec((D, GLA_DV), full),
                  pl.BlockSpec((D, 2 * GLA_R), full),
                  pl.BlockSpec((2, GLA_R, GLA_DK), full3), pl.BlockSpec((2, 1, GLA_DK), full3)],
        out_specs=out_specs, out_shape=out_shape, scratch_shapes=scratch,
        compiler_params=_cparams("parallel"),
        name="gla_in_" + ("s" if colmajor else "p"),
    )(x, shift, scale, g, wq, wk, wv, wr, wg, w2, b2.reshape(2, 1, GLA_DK))


def _gla_scan_kernel(*refs, rev, has_s0, want_sf, nchunk):
    refs = list(refs)
    q_ref, k_ref, v_ref, gl_ref = refs[:4]
    pos = 4
    s0_ref = None
    if has_s0:
        s0_ref = refs[pos]
        pos += 1
    o_ref = refs[pos]
    pos += 1
    sf_ref = None
    if want_sf:
        sf_ref = refs[pos]
        pos += 1
    st_ref = refs[pos]
    c = pl.program_id(1)
    L = GLA_L

    @pl.when(c == 0)
    def _init():
        for hh in range(GLA_H):
            if has_s0:
                st_ref[hh] = s0_ref[0, hh].T
            else:
                st_ref[hh] = jnp.zeros((GLA_HV, GLA_HK), F32)

    ii = lax.broadcasted_iota(I32, (L, L), 0)
    jj = lax.broadcasted_iota(I32, (L, L), 1)
    mask = (jj >= ii) if rev else (jj <= ii)
    gk = gl_ref[0]
    g = _mm(mask.astype(F32), gk, HI)
    ri = L - 1 - L // 2 if rev else L // 2
    ei = 0 if rev else L - 1
    gref = g[ri:ri + 1, :]
    gend = g[ei:ei + 1, :]
    qf = q_ref[0].astype(F32)
    kf = k_ref[0].astype(F32)
    qg = (qf * jnp.exp(g - gref)).astype(BF16)
    kg = (kf * jnp.exp(gref - g)).astype(BF16)
    qe = (qf * jnp.exp(g)).astype(BF16)
    ku = (kf * jnp.exp(gend - g)).astype(BF16)
    dec = jnp.exp(gend)
    v = v_ref[0]
    for hh in range(GLA_H):
        ks = slice(hh * GLA_HK, (hh + 1) * GLA_HK)
        vs = slice(hh * GLA_HV, (hh + 1) * GLA_HV)
        att = jnp.where(mask, _nt(qg[:, ks], kg[:, ks]), 0.0)
        vh = v[:, vs]
        st = st_ref[hh]
        o = _mm(att.astype(BF16), vh) + _nt(qe[:, ks], st.astype(BF16))
        o_ref[0, :, vs] = o.astype(BF16)
        ut = _mm(vh.astype(F32).T.astype(BF16), ku[:, ks])
        st_ref[hh] = st * dec[:, ks] + ut

    if want_sf:
        @pl.when(c == nchunk - 1)
        def _fin():
            for hh in range(GLA_H):
                sf_ref[0, hh] = st_ref[hh].T


def _gla_scan(q, k, v, gl, s0, *, rev, nb, nchunk, want_sf):
    has_s0 = s0 is not None
    d = 1 if rev else 0
    ch = (lambda c: nchunk - 1 - c) if rev else (lambda c: c)
    blk = lambda w: pl.BlockSpec((1, GLA_L, w), lambda b, c: (b * nchunk + ch(c), 0, 0))
    in_specs = [blk(GLA_DK), blk(GLA_DK), blk(GLA_DV),
                pl.BlockSpec((1, GLA_L, GLA_DK), lambda b, c: (b * nchunk + ch(c), 0, d))]
    args = [t.reshape(nb * nchunk, GLA_L, t.shape[-1]) for t in (q, k, v, gl)]
    if has_s0:
        in_specs.append(pl.BlockSpec((1, GLA_H, GLA_HK, GLA_HV), lambda b, c: (b, 0, 0, 0)))
        args.append(s0)
    out_specs = [blk(GLA_DV)]
    out_shape = [jax.ShapeDtypeStruct((nb * nchunk, GLA_L, GLA_DV), BF16)]
    if want_sf:
        out_specs.append(pl.BlockSpec((1, GLA_H, GLA_HK, GLA_HV), lambda b, c: (b, 0, 0, 0)))
        out_shape.append(jax.ShapeDtypeStruct((nb, GLA_H, GLA_HK, GLA_HV), F32))
    return pl.pallas_call(
        functools.partial(_gla_scan_kernel, rev=rev, has_s0=has_s0, want_sf=want_sf, nchunk=nchunk),
        grid=(nb, nchunk),
        in_specs=in_specs, out_specs=out_specs, out_shape=out_shape,
        scratch_shapes=[pltpu.VMEM((GLA_H, GLA_HV, GLA_HK), F32)],
        compiler_params=_cparams("parallel", "arbitrary"),
        name="gla_scan_" + ("b" if rev else "f") + ("_s" if has_s0 else "_p"),
    )(*args)


def _gla_out_kernel(of_ref, ob_ref, r_ref, ng_ref, wo_ref, xres_ref,
                    gate_ref, g2_ref, sc2_ref, sh2_ref, rwT_ref, rb_ref, *rest, colmajor):
    if colmajor:
        (_, _, _, _, xn_ref, h2_ref, idx_ref, gcol_ref, so_ref, sr_ref) = rest
        nc = GLA_DV // 128
        for col in range(GRID_W):
            osum = of_ref[0, col].astype(F32) + ob_ref[0, col].astype(F32)
            rcol = r_ref[0, col].astype(F32)
            for jc in range(nc):
                so_ref[jc, pl.ds(col, 8, stride=GRID_W), :] = osum[:, jc * 128:(jc + 1) * 128]
                sr_ref[jc, pl.ds(col, 8, stride=GRID_W), :] = rcol[:, jc * 128:(jc + 1) * 128]
        o = jnp.concatenate([so_ref[jc] for jc in range(nc)], axis=1)
        r = jnp.concatenate([sr_ref[jc] for jc in range(nc)], axis=1)
    else:
        xn_ref, h2_ref, idx_ref, gcol_ref = rest
        o = of_ref[...].astype(F32) + ob_ref[...].astype(F32)
        r = r_ref[...].astype(F32)
    parts = []
    for hh in range(GLA_H):
        oh = o[:, hh * GLA_HV:(hh + 1) * GLA_HV]
        parts.append(oh * lax.rsqrt(jnp.mean(oh * oh, axis=-1, keepdims=True) + EPS) * ng_ref[...])
    y = jnp.concatenate(parts, axis=1) * _silu(r)
    xn = xres_ref[...] + gate_ref[0] * _mm(y.astype(BF16), wo_ref[...])
    xn_ref[...] = xn
    _router_epilogue(xn, g2_ref[...], sc2_ref[0], sh2_ref[0], rwT_ref[...], rb_ref[...],
                     h2_ref, idx_ref, gcol_ref)


def _gla_out(of, ob, r, norm_g, wo, x, router_args, prev, *, colmajor):
    tm = 8 * GRID_W if colmajor else TM
    off = NP // tm if colmajor else 0
    ntile = (NS if colmajor else NP) // tm
    full = lambda t: (0, 0)
    if colmajor:
        blk = lambda: pl.BlockSpec((1, GRID_W, 8, GLA_DV), lambda t: (t // 8, 0, t % 8, 0))
        scratch = [pltpu.VMEM((GLA_DV // 128, tm, 128), F32), pltpu.VMEM((GLA_DV // 128, tm, 128), F32)]
    else:
        blk = lambda: pl.BlockSpec((tm, GLA_DV), lambda t: (t, 0))
        scratch = []
    in_specs = ([blk(), blk(), blk(), pl.BlockSpec((1, GLA_HV), full), pl.BlockSpec((GLA_DV, D), full),
                 pl.BlockSpec((tm, D), lambda t: (t + off, 0))] + _router_in_specs(tm, off))
    args = [of, ob, r, norm_g, wo, x, *router_args]
    aliases = {}
    if prev is not None:
        for i, p in enumerate(prev):
            in_specs.append(pl.BlockSpec(memory_space=pl.ANY))
            args.append(p)
            aliases[len(args) - 1] = i
    return pl.pallas_call(
        functools.partial(_gla_out_kernel, colmajor=colmajor),
        grid=(ntile,),
        in_specs=in_specs, out_specs=_router_out_specs(tm, off), out_shape=_ROUTER_OUT_SHAPES,
        scratch_shapes=scratch, input_output_aliases=aliases,
        compiler_params=_cparams("parallel"),
        name="gla_out_" + ("s" if colmajor else "p"),
    )(*args)


def _sc_kernel(x_ref, sh_ref, sc_ref, g_ref, wb_ref, wc_ref, wu_ref, cw_ref, wo_ref,
               gate_ref, g2_ref, sc2_ref, sh2_ref, rwT_ref, rb_ref,
               xn_ref, h2_ref, idx_ref, gcol_ref, y_scr):
    t = pl.program_id(0)
    x = x_ref[...]
    hb = _modnorm(x, g_ref[...], sc_ref[0], sh_ref[0]).astype(BF16)
    segm1 = jnp.where(t < NP // TM, SEQ_P - 1, GRID_W - 1)
    cn = 512
    for j in range(0, D, cn):
        js = slice(j, j + cn)
        gcu = _mm(hb, wc_ref[:, js]) * _mm(hb, wu_ref[:, js])
        y_scr[:, js] = (_mm(hb, wb_ref[:, js]) * _seg_conv(gcu, cw_ref[:, js], 3, segm1)).astype(BF16)
    xn = x + gate_ref[0] * _mm(y_scr[...], wo_ref[...])
    xn_ref[...] = xn
    _router_epilogue(xn, g2_ref[...], sc2_ref[0], sh2_ref[0], rwT_ref[...], rb_ref[...],
                     h2_ref, idx_ref, gcol_ref)


def _shortconv(x, shift, scale, g, wb, wc, wu, conv_w, wo, router_args):
    row = lambda t: (_cond_row(t, TM), 0, 0)
    full = lambda t: (0, 0)
    return pl.pallas_call(
        _sc_kernel,
        grid=(T // TM,),
        in_specs=[pl.BlockSpec((TM, D), lambda t: (t, 0)),
                  pl.BlockSpec((1, 1, D), row), pl.BlockSpec((1, 1, D), row),
                  pl.BlockSpec((1, D), full),
                  pl.BlockSpec((D, D), full), pl.BlockSpec((D, D), full), pl.BlockSpec((D, D), full),
                  pl.BlockSpec((3, D), full), pl.BlockSpec((D, D), full)] + _router_in_specs(TM, 0),
        out_specs=_router_out_specs(TM, 0), out_shape=_ROUTER_OUT_SHAPES,
        scratch_shapes=[pltpu.VMEM((TM, D), BF16)],
        compiler_params=_cparams("parallel"),
        name="shortconv",
    )(x, shift, scale, g, wb, wc, wu, conv_w, wo, *router_args)


def _to_rows(tile_ref, m):
    return jnp.concatenate([tile_ref[pl.ds(s, m, stride=SUB), :] for s in range(SUB)], axis=1)


def _store_tiles(tile_ref, val, m):
    for s in range(SUB):
        tile_ref[pl.ds(s, m, stride=SUB), :] = val[:, s * 128:(s + 1) * 128]


def _moe_kernel(n_ref, ie_ref, chg_ref,
                tok0_ref, tokn_ref, dstp_ref, dstc_ref,
                h_hbm, wgu_ref, bgu_ref, wd_ref, bd_ref, y_hbm,
                xbuf, obuf, act_scr, wgu_bf, wd_bf, gsem, ssem):
    i = pl.program_id(0)
    n = n_ref[0]
    slot = i % 2
    oslot = 1 - slot

    def gather(idx_ref, r, s):
        src = h_hbm.at[pl.ds(pl.multiple_of(idx_ref[0, 0, r], SUB), SUB), :]
        return pltpu.make_async_copy(src, xbuf.at[s, pl.ds(pl.multiple_of(r * SUB, SUB), SUB), :], gsem.at[s])

    def scatter(idx_ref, r, s):
        dst = y_hbm.at[pl.ds(pl.multiple_of(idx_ref[0, 0, r], SUB), SUB), :]
        return pltpu.make_async_copy(obuf.at[s, pl.ds(pl.multiple_of(r * SUB, SUB), SUB), :], dst, ssem.at[s])

    def wait_gather(s):
        pltpu.make_async_copy(h_hbm.at[pl.ds(0, BM * SUB), :], xbuf.at[s], gsem.at[s]).wait()

    def wait_scatter(s):
        pltpu.make_async_copy(obuf.at[s], y_hbm.at[pl.ds(0, BM * SUB), :], ssem.at[s]).wait()

    @pl.when((i < n) & (chg_ref[i] == 1))
    def _cast_weights():
        wgu_bf[...] = wgu_ref[0, 0].astype(BF16)
        wd_bf[...] = wd_ref[0, 0].astype(BF16)

    @pl.when(i == 0)
    def _prologue():
        obuf[1] = jnp.zeros((BM * SUB, 128), F32)

        def body(r, c):
            gather(tok0_ref, r, 0).start()
            return c
        lax.fori_loop(0, BM, body, 0)

    @pl.when((i >= 1) & (i < n))
    def _free_obuf():
        wait_scatter(slot)

    @pl.when(i < n)
    def _main():
        wait_gather(slot)
        xb = _to_rows(xbuf.at[slot], BM).astype(BF16)
        half = DFF // 2
        per = BM // 4
        for q in range(4):
            for r in range(q * per, (q + 1) * per):
                gather(tokn_ref, r, oslot).start()
                scatter(dstp_ref, r, oslot).start()
            if q < 2:
                cs = slice(q * half, (q + 1) * half)
                us = slice(DFF + q * half, DFF + (q + 1) * half)
                gg = jnp.minimum(_mm(xb, wgu_bf[:, cs]) + bgu_ref[0, 0, :, cs], LIMIT)
                uu = jnp.clip(_mm(xb, wgu_bf[:, us]) + bgu_ref[0, 0, :, us], -LIMIT, LIMIT)
                act_scr[:, cs] = (gg * _sigmoid(ALPHA * gg) * (uu + 1.0)).astype(BF16)
            else:
                cs = slice((q - 2) * half, (q - 1) * half)
                yo = _mm(act_scr[...], wd_bf[:, cs]) + bd_ref[0, 0, :, cs]
                for s in range(SUB // 2):
                    sg = (q - 2) * (SUB // 2) + s
                    obuf[slot, pl.ds(sg, BM, stride=SUB), :] = yo[:, s * 128:(s + 1) * 128]

    @pl.when(i == n - 1)
    def _drain():
        wait_scatter(oslot)

        def body(r, c):
            scatter(dstc_ref, r, slot).start()
            return c
        lax.fori_loop(0, BM, body, 0)
        wait_scatter(slot)
        wait_gather(oslot)


def _moe_experts(plan, h2t, w_gu, b_gu, w_down, b_down, layer):
    n, ie, chg, tok8, dst8 = plan
    smem = lambda f: pl.BlockSpec((1, 1, BM), f, memory_space=pltpu.SMEM)
    wspec = lambda shape: pl.BlockSpec(shape, lambda i, n, ie, chg: (layer, ie[i], 0, 0))
    grid_spec = pltpu.PrefetchScalarGridSpec(
        num_scalar_prefetch=3,
        grid=(NB_TOT,),
        in_specs=[smem(lambda i, *_: (0, 0, 0)),
                  smem(lambda i, *_: (i + 1, 0, 0)),
                  smem(lambda i, *_: (i, 0, 0)),
                  smem(lambda i, *_: (i + 1, 0, 0)),
                  pl.BlockSpec(memory_space=pl.ANY),
                  wspec((1, 1, D, 2 * DFF)), wspec((1, 1, 1, 2 * DFF)),
                  wspec((1, 1, DFF, D)), wspec((1, 1, 1, D))],
        out_specs=pl.BlockSpec(memory_space=pl.ANY),
        scratch_shapes=[pltpu.VMEM((2, BM * SUB, 128), F32), pltpu.VMEM((2, BM * SUB, 128), F32),
                        pltpu.VMEM((BM, DFF), BF16),
                        pltpu.VMEM((D, 2 * DFF), BF16), pltpu.VMEM((DFF, D), BF16),
                        pltpu.SemaphoreType.DMA((2,)), pltpu.SemaphoreType.DMA((2,))],
    )
    return pl.pallas_call(
        _moe_kernel,
        grid_spec=grid_spec,
        out_shape=jax.ShapeDtypeStruct((Y_ROWS * SUB, 128), F32),
        compiler_params=_cparams("arbitrary"),
        name="moe_experts",
    )(n, ie, chg, tok8, tok8, dst8, dst8, h2t, w_gu, b_gu.reshape(DEPTH, NE, 1, 2 * DFF),
      w_down, b_down.reshape(DEPTH, NE, 1, D))


def _moe_plan(idx_t):
    e_flat = idx_t.reshape(NAS)
    order = jnp.argsort(e_flat).astype(I32)
    counts = jnp.sum((e_flat[None, :] == jnp.arange(NE, dtype=I32)[:, None]).astype(I32), axis=1)
    nblk_e = (counts + BM - 1) // BM
    bend = jnp.cumsum(nblk_e).astype(I32)
    bstart = bend - nblk_e
    off_end = jnp.cumsum(counts).astype(I32)
    off = off_end - counts
    n = bend[-1]
    blk = jnp.arange(NB_TOT + 1, dtype=I32)
    be = jnp.minimum(jnp.sum((blk[:, None] >= bend[None, :]).astype(I32), axis=1), NE - 1)
    real = blk < n
    r = jnp.arange(BM, dtype=I32)[None, :]
    p = (off[be] + (blk - bstart[be]) * BM)[:, None] + r
    valid = real[:, None] & (p < off_end[be][:, None])
    a = order[jnp.clip(p, 0, NAS - 1)]
    tok8 = jnp.where(valid, (a % T) * SUB, 0)
    pad8 = (NAS + (blk % 2)[:, None] * BM + r) * SUB
    dst8 = jnp.where(valid, a * SUB, pad8)
    dst8 = jnp.concatenate([(NAS + BM + r) * SUB, dst8[:-1]], axis=0)
    ie = jnp.where(real, be, be[jnp.maximum(n - 1, 0)])[:NB_TOT]
    chg = (ie != jnp.concatenate([jnp.full((1,), -1, I32), ie[:-1]])).astype(I32)
    return (n.reshape(1), ie, chg, tok8.reshape(NB_TOT + 1, 1, BM), dst8.reshape(NB_TOT + 1, 1, BM))


def _combine_kernel(x_ref, y0_ref, y1_ref, y2_ref, y3_ref, gcol_ref, gate_ref, *rest, final):
    if final:
        fg_ref, o_ref = rest
    else:
        (o_ref,) = rest
    gc = gcol_ref[...]
    m = x_ref.shape[0]
    acc = _to_rows(y0_ref, m) * gc[:, 0:1]
    for k, y_ref in enumerate((y1_ref, y2_ref, y3_ref), start=1):
        acc = acc + _to_rows(y_ref, m) * gc[:, k:k + 1]
    xn = x_ref[...] + gate_ref[0] * acc
    if final:
        xn = xn * lax.rsqrt(jnp.mean(xn * xn, axis=-1, keepdims=True) + EPS) * fg_ref[...]
    o_ref[...] = xn


def _combine(x, y4, gcol, gate2, final_g):
    final = final_g is not None
    nt = T // TM
    yspec = lambda k: pl.BlockSpec((TM * SUB, 128), lambda t: (k * nt + t, 0))
    in_specs = [pl.BlockSpec((TM, D), lambda t: (t, 0)),
                yspec(0), yspec(1), yspec(2), yspec(3),
                pl.BlockSpec((TM, 128), lambda t: (t, 0)),
                pl.BlockSpec((1, 1, D), lambda t: (_cond_row(t, TM), 0, 0))]
    args = [x, y4, y4, y4, y4, gcol, gate2]
    if final:
        in_specs.append(pl.BlockSpec((1, D), lambda t: (0, 0)))
        args.append(final_g)
    return pl.pallas_call(
        functools.partial(_combine_kernel, final=final),
        grid=(nt,),
        in_specs=in_specs,
        out_specs=pl.BlockSpec((TM, D), lambda t: (t, 0)),
        out_shape=jax.ShapeDtypeStruct((T, D), F32),
        compiler_params=_cparams("parallel"),
        name="moe_combine",
    )(*args)


def kernel(x_prompt, x_sample, state_ssd, state_gla, c, c_ctx, w_mod, b_mod, norm1_g, norm2_g, ssd_w_in, ssd_conv_w, ssd_conv_b, ssd_dt_bias, ssd_a_log, ssd_d, ssd_norm_g, ssd_w_out, gla_w_in, gla_w_gate2, gla_b_gate2, gla_norm_g, gla_w_out, sc_w_in, sc_conv_w, sc_w_out, router_w, router_b, moe_w_gu, moe_b_gu, moe_w_down, moe_b_down, final_norm_g):
    x = jnp.concatenate([x_prompt.reshape(NP, D), x_sample.reshape(NS, D)], axis=0)
    cond8 = jnp.concatenate([c_ctx[None, :], c, jnp.zeros((8 - 1 - NB_S, D), F32)], axis=0)
    mod = _modulation(cond8, w_mod, b_mod)

    new_ssd, new_gla = [], []
    for i in range(DEPTH):
        m6 = [mod[i, :, j * D:(j + 1) * D].reshape(8, 1, D) for j in range(6)]
        shift1, scale1, gate1, shift2, scale2, gate2 = m6
        n1 = norm1_g[i].reshape(1, D)
        router_args = (gate1, norm2_g[i].reshape(1, D), scale2, shift2,
                       router_w[i].T, router_b[i].reshape(NE, 1))
        kind, j = i % 3, i // 3
        if kind == 0:
            w_in = ssd_w_in[j]
            wz = w_in[:, :DI].astype(BF16)
            wx = w_in[:, DI:DI + SSD_XBC].astype(BF16)
            wdt = w_in[:, DI + SSD_XBC:]
            z, xbc, dt, dtT = _ssd_in(x, shift1, scale1, n1, wz, wx, wdt, ssd_conv_w[j], ssd_conv_b[j],
                                      ssd_dt_bias[j].reshape(-1))
            a = -jnp.exp(ssd_a_log[j].astype(F32))
            ys, hfs = [], []
            for rev in (False, True):
                d = 1 if rev else 0
                yp, hf = _ssd_scan(xbc, dt, dtT, a, None, None, rev=rev, nb=NB_P,
                                   nchunk=SEQ_P // SSD_L, base=0, want_hf=True)
                h0 = state_ssd[:, j, d].reshape(NB_S, DI, SSD_N)
                (yd,) = _ssd_scan(xbc, dt, dtT, a, h0, yp, rev=rev, nb=NB_S,
                                  nchunk=SEQ_S // SSD_L, base=NP, want_hf=False)
                ys.append(yd)
                hfs.append(hf.reshape(NB_P, SSD_H, SSD_P, SSD_N))
            new_ssd.append(jnp.stack(hfs, axis=1))
            dskip_x = jnp.repeat(ssd_d[j].astype(F32), SSD_P).reshape(1, DI)
            x, h2, idx_t, gcol = _ssd_out(ys[0], ys[1], xbc, z, dskip_x, ssd_norm_g[j].reshape(1, DI),
                                          ssd_w_out[j].astype(BF16), x, router_args)
        elif kind == 1:
            w_in = gla_w_in[j]
            wq = w_in[:, :GLA_DK].astype(BF16)
            wk = w_in[:, GLA_DK:2 * GLA_DK].astype(BF16)
            wv = w_in[:, 2 * GLA_DK:2 * GLA_DK + GLA_DV].astype(BF16)
            wr = w_in[:, 2 * GLA_DK + GLA_DV:2 * GLA_DK + 2 * GLA_DV].astype(BF16)
            wg = w_in[:, 2 * GLA_DK + 2 * GLA_DV:]
            wo = gla_w_out[j].astype(BF16)
            ng = gla_norm_g[j].reshape(1, GLA_HV)
            prev = None
            sfs = []
            for colmajor in (False, True):
                q, k, v, r, gl = _gla_in(x, shift1, scale1, n1, wq, wk, wv, wr, wg,
                                         gla_w_gate2[j], gla_b_gate2[j], colmajor=colmajor)
                nb = NB_S if colmajor else NB_P
                nchunk = (SEQ_S if colmajor else SEQ_P) // GLA_L
                os_ = []
                for rev in (False, True):
                    d = 1 if rev else 0
                    s0 = state_gla[:, j, d] if colmajor else None
                    res = _gla_scan(q, k, v, gl, s0, rev=rev, nb=nb, nchunk=nchunk, want_sf=not colmajor)
                    o = res[0]
                    if colmajor:
                        o = o.reshape(NB_S, GRID_W, GRID_W, GLA_DV)
                    else:
                        o = o.reshape(NP, GLA_DV)
                        sfs.append(res[1])
                    os_.append(o)
                prev = _gla_out(os_[0], os_[1], r, ng, wo, x, router_args, prev, colmajor=colmajor)
            new_gla.append(jnp.stack(sfs, axis=1))
            x, h2, idx_t, gcol = prev
        else:
            w_in = sc_w_in[j]
            x, h2, idx_t, gcol = _shortconv(x, shift1, scale1, n1, w_in[:, :D].astype(BF16),
                                            w_in[:, D:2 * D].astype(BF16), w_in[:, 2 * D:].astype(BF16),
                                            sc_conv_w[j], sc_w_out[j].astype(BF16), router_args)
        y4 = _moe_experts(_moe_plan(idx_t), h2, moe_w_gu, moe_b_gu, moe_w_down, moe_b_down, i)
        x = _combine(x, y4, gcol, gate2, final_norm_g.reshape(1, D) if i == DEPTH - 1 else None)

    y_prompt = x[:NP].reshape(NB_P, SEQ_P, D)
    y_sample = x[NP:].reshape(NB_S, SEQ_S, D)
    new_state_ssd = jnp.stack(new_ssd, axis=1)
    new_state_gla = jnp.stack(new_gla, axis=1)
    return (y_prompt, y_sample, new_state_ssd, new_state_gla)
```

```python
import functools

import jax
import jax.numpy as jnp
from jax import lax
from jax.experimental import pallas as pl
from jax.experimental.pallas import tpu as pltpu

F32 = jnp.float32
BF16 = jnp.bfloat16
I32 = jnp.int32
HI = lax.Precision.HIGHEST

D = 1024
NB_P, SEQ_P = 32, 256
NB_S, SEQ_S = 4, 4096
GRID_W = 64
DEPTH = 4
EPS = 1e-6
NP = NB_P * SEQ_P
NS = NB_S * SEQ_S
T = NP + NS

DI = 2 * D
SSD_P = 64
SSD_H = DI // SSD_P
SSD_G = 4
SSD_N = 128
SSD_L = 128
SSD_XBC = DI + 2 * SSD_G * SSD_N
GLA_H = 4
GLA_DK = D // 2
GLA_DV = D
GLA_HK = GLA_DK // GLA_H
GLA_HV = GLA_DV // GLA_H
GLA_R = 16
GLA_NORM = 16.0
GLA_L = 64
NE = 32
TOPK = 4
DFF = D
ALPHA = 1.702
LIMIT = 7.0
NAS = T * TOPK

TM = 256
BM = 256
NB_TOT = NAS // BM + NE
Y_ROWS = NAS + 2 * BM
NSLOT = 3
SUB = 8
VMEM_LIMIT = 56 * 1024 * 1024


def _cparams(*sem):
    return pltpu.CompilerParams(dimension_semantics=sem, vmem_limit_bytes=VMEM_LIMIT)


def _nt(a, b, precision=None):
    return lax.dot_general(a, b, (((1,), (1,)), ((), ())), precision=precision,
                           preferred_element_type=F32)


def _mm(a, b, precision=None):
    return jnp.dot(a, b, precision=precision, preferred_element_type=F32)


def _split3(v):
    hi = v.astype(BF16)
    r1 = v - hi.astype(F32)
    mid = r1.astype(BF16)
    lo = (r1 - mid.astype(F32)).astype(BF16)
    return jnp.concatenate([hi, mid, lo], axis=1)


def _sigmoid(x):
    return 1.0 / (1.0 + jnp.exp(-x))


def _silu(x):
    return x * _sigmoid(x)


def _softplus(x):
    return jnp.maximum(x, 0.0) + jnp.log1p(jnp.exp(-jnp.abs(x)))


def _modnorm(x, g, scale, shift):
    y = x * lax.rsqrt(jnp.mean(x * x, axis=-1, keepdims=True) + EPS) * g
    return y * (1.0 + scale) + shift


def _cond_row(t, tm):
    npt = NP // tm
    return jnp.where(t < npt, 0, 1 + (t - npt) // (SEQ_S // tm))


def _seg_conv(acc, w, width, segm1):
    m = acc.shape[0]
    pos = lax.broadcasted_iota(I32, (m, 1), 0) & segm1
    half = width // 2
    out = acc * w[half:half + 1, :]
    for k in range(width):
        s = k - half
        if s == 0:
            continue
        rolled = pltpu.roll(acc, (m - s) % m, 0)
        valid = ((pos + s >= 0) & (pos + s <= segm1)).astype(F32)
        out = out + (rolled * valid) * w[k:k + 1, :]
    return out


def _mod_kernel(c_ref, w_ref, b_ref, o_ref):
    o_ref[0] = _mm(_silu(c_ref[...]), w_ref[0], HI) + b_ref[0]


def _modulation(cond8, w_mod, b_mod):
    nj = 6
    return pl.pallas_call(
        _mod_kernel,
        grid=(DEPTH, nj),
        in_specs=[pl.BlockSpec((8, D), lambda l, j: (0, 0)),
                  pl.BlockSpec((1, D, D), lambda l, j: (l, 0, j)),
                  pl.BlockSpec((1, 1, D), lambda l, j: (l, 0, j))],
        out_specs=pl.BlockSpec((1, 8, D), lambda l, j: (l, 0, j)),
        out_shape=jax.ShapeDtypeStruct((DEPTH, 8, 6 * D), F32),
        compiler_params=_cparams("parallel", "parallel"),
        name="modulation",
    )(cond8, w_mod, b_mod.reshape(DEPTH, 1, 6 * D))


def _router_epilogue(xn, g2, sc2, sh2, rwT, rb, h2_ref, idx_ref, gcol_ref):
    m = xn.shape[0]
    h2 = _modnorm(xn, g2, sc2, sh2)
    _store_tiles(h2_ref, h2, m)
    lg = _nt(rwT, h2, HI) + rb
    eidx = lax.broadcasted_iota(I32, (NE, m), 0)
    vals, idxs = [], []
    for _ in range(TOPK):
        mx = jnp.max(lg, axis=0, keepdims=True)
        sel = jnp.min(jnp.where(lg == mx, eidx, NE), axis=0, keepdims=True)
        vals.append(mx)
        idxs.append(sel)
        lg = jnp.where(eidx == sel, -jnp.inf, lg)
    ex = [jnp.exp(v - vals[0]) for v in vals]
    den = ex[0] + ex[1] + ex[2] + ex[3]
    row4 = lax.broadcasted_iota(I32, (TOPK, m), 0)
    idx = jnp.zeros((TOPK, m), I32)
    for k in range(TOPK):
        idx = jnp.where(row4 == k, idxs[k], idx)
    idx_ref[...] = idx
    row = lax.broadcasted_iota(I32, (128, m), 0)
    slab = jnp.zeros((128, m), F32)
    for k in range(TOPK):
        slab = jnp.where(row == k, ex[k] / den, slab)
    gcol_ref[...] = slab.T


_ROUTER_OUT_SHAPES = (jax.ShapeDtypeStruct((T, D), F32),
                      jax.ShapeDtypeStruct((T * SUB, 128), F32),
                      jax.ShapeDtypeStruct((TOPK, T), I32),
                      jax.ShapeDtypeStruct((T, 128), F32))


def _router_out_specs(tm, off):
    return [pl.BlockSpec((tm, D), lambda t: (t + off, 0)),
            pl.BlockSpec((tm * SUB, 128), lambda t: (t + off, 0)),
            pl.BlockSpec((TOPK, tm), lambda t: (0, t + off)),
            pl.BlockSpec((tm, 128), lambda t: (t + off, 0))]


def _router_in_specs(tm, off):
    row = lambda t: (_cond_row(t + off, tm), 0, 0)
    full = lambda t: (0, 0)
    return [pl.BlockSpec((1, 1, D), row),
            pl.BlockSpec((1, D), full),
            pl.BlockSpec((1, 1, D), row),
            pl.BlockSpec((1, 1, D), row),
            pl.BlockSpec((NE, D), full),
            pl.BlockSpec((NE, 1), full)]


def _ssd_in_kernel(x_ref, sh_ref, sc_ref, g_ref, wz_ref, wx_ref, wdt_ref, wdtT_ref, cw_ref, cb_ref,
                   dtb_ref, dtbT_ref, z_ref, xbc_ref, dt_ref, dtT_ref):
    t = pl.program_id(0)
    h = _modnorm(x_ref[...], g_ref[...], sc_ref[0], sh_ref[0])
    hb = h.astype(BF16)
    cn = 512
    for j in range(0, DI, cn):
        z_ref[:, j:j + cn] = _mm(hb, wz_ref[:, j:j + cn]).astype(BF16)
    segm1 = jnp.where(t < NP // TM, SEQ_P - 1, GRID_W - 1)
    for j in range(0, SSD_XBC, cn):
        acc = _mm(hb, wx_ref[:, j:j + cn])
        out = _seg_conv(acc, cw_ref[:, j:j + cn], 5, segm1) + cb_ref[:, j:j + cn]
        xbc_ref[:, j:j + cn] = _silu(out).astype(BF16)
    dt_ref[...] = _softplus(_mm(h, wdt_ref[...], HI) + dtb_ref[...])
    dtT_ref[...] = _softplus(_nt(wdtT_ref[...], h, HI) + dtbT_ref[...])


def _ssd_in(x, shift, scale, g, wz, wx, wdt, conv_w, conv_b, dt_bias):
    row = lambda t: (_cond_row(t, TM), 0, 0)
    full = lambda t: (0, 0)
    nh2 = 2 * SSD_H
    return pl.pallas_call(
        _ssd_in_kernel,
        grid=(T // TM,),
        in_specs=[pl.BlockSpec((TM, D), lambda t: (t, 0)),
                  pl.BlockSpec((1, 1, D), row), pl.BlockSpec((1, 1, D), row),
                  pl.BlockSpec((1, D), full),
                  pl.BlockSpec((D, DI), full), pl.BlockSpec((D, SSD_XBC), full),
                  pl.BlockSpec((D, nh2), full), pl.BlockSpec((nh2, D), full),
                  pl.BlockSpec((5, SSD_XBC), full), pl.BlockSpec((1, SSD_XBC), full),
                  pl.BlockSpec((1, nh2), full), pl.BlockSpec((nh2, 1), full)],
        out_specs=[pl.BlockSpec((TM, DI), lambda t: (t, 0)),
                   pl.BlockSpec((TM, SSD_XBC), lambda t: (t, 0)),
                   pl.BlockSpec((TM, nh2), lambda t: (t, 0)),
                   pl.BlockSpec((nh2, TM), lambda t: (0, t))],
        out_shape=[jax.ShapeDtypeStruct((T, DI), BF16), jax.ShapeDtypeStruct((T, SSD_XBC), BF16),
                   jax.ShapeDtypeStruct((T, nh2), F32), jax.ShapeDtypeStruct((nh2, T), F32)],
        compiler_params=_cparams("parallel"),
        name="ssd_in",
    )(x, shift, scale, g, wz, wx, wdt, wdt.T, conv_w, conv_b.reshape(1, -1),
      dt_bias.reshape(1, nh2), dt_bias.reshape(nh2, 1))


def _ssd_scan_kernel(*refs, rev, has_h0, want_hf, nchunk):
    refs = list(refs)
    xbc_ref, dt_ref, dtT_ref, a_ref, aT_ref = refs[:5]
    pos = 5
    h0_ref = None
    if has_h0:
        h0_ref = refs[pos]
        pos += 1
    y_ref = refs[pos]
    pos += 1
    hf_ref = None
    if want_hf:
        hf_ref = refs[pos]
        pos += 1
    s_ref = refs[pos]
    c = pl.program_id(1)
    L = SSD_L
    d0 = SSD_H if rev else 0

    @pl.when(c == 0)
    def _init():
        if has_h0:
            s_ref[...] = h0_ref[0].T
        else:
            s_ref[...] = jnp.zeros_like(s_ref)

    ii = lax.broadcasted_iota(I32, (L, L), 0)
    jj = lax.broadcasted_iota(I32, (L, L), 1)
    mask = (jj >= ii) if rev else (jj <= ii)
    tri = mask.astype(F32)
    dt = dt_ref[:, d0:d0 + SSD_H]
    dtT = dtT_ref[...]
    acum = _mm(tri, dt * a_ref[...], HI)
    acumT = _nt(dtT * aT_ref[...], tri, HI)
    last = 0 if rev else L - 1
    tot = acum[last:last + 1, :]
    expand = ((lax.broadcasted_iota(I32, (3 * SSD_H, DI), 1) >> 6)
              == (lax.broadcasted_iota(I32, (3 * SSD_H, DI), 0) & (SSD_H - 1))).astype(BF16)
    eacum_x = _mm(_split3(jnp.exp(acum)), expand)
    wcol_x = _mm(_split3(jnp.exp(tot - acum) * dt), expand)
    etot_x = _mm(_split3(jnp.broadcast_to(jnp.exp(tot), (8, SSD_H))), expand)[0:1, :]
    lane = lax.broadcasted_iota(I32, (L, 128), 1)
    for g in range(SSD_G):
        gs = slice(g * 512, (g + 1) * 512)
        bg = xbc_ref[:, DI + g * SSD_N:DI + (g + 1) * SSD_N]
        cg = xbc_ref[:, DI + SSD_G * SSD_N + g * SSD_N:DI + SSD_G * SSD_N + (g + 1) * SSD_N]
        cb = _nt(cg, bg)
        sg = s_ref[:, gs]
        yint = _mm(cg, sg.astype(BF16))
        xg = xbc_ref[:, gs]
        for p in range(4):
            ms = []
            for hh in (g * 8 + 2 * p, g * 8 + 2 * p + 1):
                seg = acum[:, hh:hh + 1] - acumT[hh:hh + 1, :]
                dec = jnp.exp(jnp.where(mask, seg, -1e30))
                ms.append((cb * dec * dtT[hh:hh + 1, :]).astype(BF16))
            lhs = jnp.concatenate(ms, axis=1)
            xp = xg[:, p * 128:(p + 1) * 128]
            rhs = jnp.concatenate([jnp.where(lane < SSD_P, xp, jnp.zeros_like(xp)),
                                   jnp.where(lane >= SSD_P, xp, jnp.zeros_like(xp))], axis=0)
            cs = slice(g * 512 + p * 128, g * 512 + (p + 1) * 128)
            yp = _mm(lhs, rhs) + yint[:, p * 128:(p + 1) * 128] * eacum_x[:, cs]
            y_ref[:, cs] = yp.astype(BF16)
        xw = (xg.astype(F32) * wcol_x[:, gs]).astype(BF16)
        st = _mm(bg.astype(F32).T.astype(BF16), xw)
        s_ref[:, gs] = sg * etot_x[:, gs] + st

    if want_hf:
        @pl.when(c == nchunk - 1)
        def _fin():
            hf_ref[0] = s_ref[...].T


def _ssd_scan(xbc, dt, dtT, a, h0, y_prev, *, rev, nb, nchunk, base, want_hf):
    has_h0 = h0 is not None
    d = 1 if rev else 0
    cbase = base // SSD_L

    def tok(b, c):
        return cbase + b * nchunk + (nchunk - 1 - c if rev else c)

    in_specs = [pl.BlockSpec((SSD_L, SSD_XBC), lambda b, c: (tok(b, c), 0)),
                pl.BlockSpec((SSD_L, 2 * SSD_H), lambda b, c: (tok(b, c), 0)),
                pl.BlockSpec((SSD_H, SSD_L), lambda b, c: (d, tok(b, c))),
                pl.BlockSpec((1, SSD_H), lambda b, c: (0, 0)),
                pl.BlockSpec((SSD_H, 1), lambda b, c: (0, 0))]
    args = [xbc, dt, dtT, a[d].reshape(1, SSD_H), a[d].reshape(SSD_H, 1)]
    if has_h0:
        in_specs.append(pl.BlockSpec((1, DI, SSD_N), lambda b, c: (b, 0, 0)))
        args.append(h0)
    out_specs = [pl.BlockSpec((SSD_L, DI), lambda b, c: (tok(b, c), 0))]
    out_shape = [jax.ShapeDtypeStruct((T, DI), BF16)]
    if want_hf:
        out_specs.append(pl.BlockSpec((1, DI, SSD_N), lambda b, c: (b, 0, 0)))
        out_shape.append(jax.ShapeDtypeStruct((nb, DI, SSD_N), F32))
    aliases = {}
    if y_prev is not None:
        in_specs.append(pl.BlockSpec(memory_space=pl.ANY))
        args.append(y_prev)
        aliases = {len(args) - 1: 0}

    def body(*refs):
        refs = list(refs)
        if y_prev is not None:
            n_in = len(args)
            refs = refs[:n_in - 1] + refs[n_in:]
        _ssd_scan_kernel(*refs, rev=rev, has_h0=has_h0, want_hf=want_hf, nchunk=nchunk)

    return pl.pallas_call(
        body,
        grid=(nb, nchunk),
        in_specs=in_specs, out_specs=out_specs, out_shape=out_shape,
        scratch_shapes=[pltpu.VMEM((SSD_N, DI), F32)],
        input_output_aliases=aliases,
        compiler_params=_cparams("parallel", "arbitrary"),
        name="ssd_scan_" + ("b" if rev else "f") + ("_s" if has_h0 else "_p"),
    )(*args)


def _ssd_out_kernel(yf_ref, yb_ref, x2_ref, z_ref, dsk_ref, ng_ref, wo_ref, xres_ref,
                    gate_ref, g2_ref, sc2_ref, sh2_ref, rwT_ref, rb_ref,
                    xn_ref, h2_ref, idx_ref, gcol_ref):
    y = yf_ref[...].astype(F32) + yb_ref[...].astype(F32) + dsk_ref[...] * x2_ref[...].astype(F32)
    y = y * _silu(z_ref[...].astype(F32))
    y = y * lax.rsqrt(jnp.mean(y * y, axis=-1, keepdims=True) + EPS) * ng_ref[...]
    xn = xres_ref[...] + gate_ref[0] * _mm(y.astype(BF16), wo_ref[...])
    xn_ref[...] = xn
    _router_epilogue(xn, g2_ref[...], sc2_ref[0], sh2_ref[0], rwT_ref[...], rb_ref[...],
                     h2_ref, idx_ref, gcol_ref)


def _ssd_out(yf, yb, xbc, z, dskip_x, norm_g, wo, x, router_args):
    full = lambda t: (0, 0)
    rowb = lambda t: (t, 0)
    return pl.pallas_call(
        _ssd_out_kernel,
        grid=(T // TM,),
        in_specs=[pl.BlockSpec((TM, DI), rowb), pl.BlockSpec((TM, DI), rowb),
                  pl.BlockSpec((TM, DI), rowb), pl.BlockSpec((TM, DI), rowb),
                  pl.BlockSpec((1, DI), full), pl.BlockSpec((1, DI), full),
                  pl.BlockSpec((DI, D), full), pl.BlockSpec((TM, D), rowb)] + _router_in_specs(TM, 0),
        out_specs=_router_out_specs(TM, 0),
        out_shape=_ROUTER_OUT_SHAPES,
        compiler_params=_cparams("parallel"),
        name="ssd_out",
    )(yf, yb, xbc, z, dskip_x, norm_g, wo, x, *router_args)


def _gla_in_kernel(x_ref, sh_ref, sc_ref, g_ref, wq_ref, wk_ref, wv_ref, wr_ref, wg_ref, w2_ref, b2_ref,
                   q_ref, k_ref, v_ref, r_ref, gl_ref, *scratch, colmajor):
    h = _modnorm(x_ref[...], g_ref[...], sc_ref[0], sh_ref[0])
    hb = h.astype(BF16)
    glow = _mm(h, wg_ref[...], HI)

    def gate(zdir, cs):
        pre = _mm(glow[:, zdir * GLA_R:(zdir + 1) * GLA_R], w2_ref[zdir, :, cs], HI) + b2_ref[zdir, :, cs]
        return (jnp.minimum(pre, 0.0) - jnp.log1p(jnp.exp(-jnp.abs(pre)))) / GLA_NORM

    prods = [(q_ref, GLA_DK, lambda cs: _mm(hb, wq_ref[:, cs]) * (GLA_HK ** -0.5)),
             (k_ref, GLA_DK, lambda cs: _mm(hb, wk_ref[:, cs])),
             (v_ref, GLA_DV, lambda cs: _mm(hb, wv_ref[:, cs])),
             (r_ref, GLA_DV, lambda cs: _mm(hb, wr_ref[:, cs])),
             (gl_ref, GLA_DK, lambda cs: gate(0, cs)),
             (gl_ref, GLA_DK, lambda cs: gate(1, cs))]
    cn = 256
    if not colmajor:
        for n_out, (o_ref, width, fn) in enumerate(prods):
            base = GLA_DK if n_out == 5 else 0
            for c in range(0, width, cn):
                o_ref[:, base + c:base + c + cn] = fn(slice(c, c + cn)).astype(o_ref.dtype)
    else:
        (scr,) = scratch
        for n_out, (o_ref, width, fn) in enumerate(prods):
            base = GLA_DK if n_out == 5 else 0
            for c in range(0, width, cn):
                val = fn(slice(c, c + cn))
                for jc in range(cn // 128):
                    scr[jc] = val[:, jc * 128:(jc + 1) * 128]
                for jc in range(cn // 128):
                    lo = base + c + jc * 128
                    for col in range(GRID_W):
                        o_ref[0, col, :, lo:lo + 128] = scr[jc, pl.ds(col, 8, stride=GRID_W), :].astype(o_ref.dtype)


def _gla_in(x, shift, scale, g, wq, wk, wv, wr, wg, w2, b2, *, colmajor):
    tm = 8 * GRID_W if colmajor else TM
    off = NP // tm if colmajor else 0
    ntile = (NS if colmajor else NP) // tm
    row = lambda t: (_cond_row(t + off, tm), 0, 0)
    full = lambda t: (0, 0)
    full3 = lambda t: (0, 0, 0)
    widths = [GLA_DK, GLA_DK, GLA_DV, GLA_DV, 2 * GLA_DK]
    dts = [BF16, BF16, BF16, BF16, F32]
    if colmajor:
        out_specs = [pl.BlockSpec((1, GRID_W, 8, w), lambda t: (t // 8, 0, t % 8, 0)) for w in widths]
        out_shape = [jax.ShapeDtypeStruct((NB_S, GRID_W, GRID_W, w), dt) for w, dt in zip(widths, dts)]
        scratch = [pltpu.VMEM((2, tm, 128), F32)]
    else:
        out_specs = [pl.BlockSpec((tm, w), lambda t: (t, 0)) for w in widths]
        out_shape = [jax.ShapeDtypeStruct((NP, w), dt) for w, dt in zip(widths, dts)]
        scratch = []
    return pl.pallas_call(
        functools.partial(_gla_in_kernel, colmajor=colmajor),
        grid=(ntile,),
        in_specs=[pl.BlockSpec((tm, D), lambda t: (t + off, 0)),
                  pl.BlockSpec((1, 1, D), row), pl.BlockSpec((1, 1, D), row),
                  pl.BlockSpec((1, D), full),
                  pl.BlockSpec((D, GLA_DK), full), pl.BlockSpec((D, GLA_DK), full),
                  pl.BlockSpec((D, GLA_DV), full), pl.BlockSpec((D, GLA_DV), full),
                  pl.BlockSpec((D, 2 * GLA_R), full),
                  pl.BlockSpec((2, GLA_R, GLA_DK), full3), pl.BlockSpec((2, 1, GLA_DK), full3)],
        out_specs=out_specs, out_shape=out_shape, scratch_shapes=scratch,
        compiler_params=_cparams("parallel"),
        name="gla_in_" + ("s" if colmajor else "p"),
    )(x, shift, scale, g, wq, wk, wv, wr, wg, w2, b2.reshape(2, 1, GLA_DK))


def _gla_dir_step(q_ref, k_ref, v_ref, gl_ref, o_ref, st_ref, d, rev):
    L = GLA_L
    ii = lax.broadcasted_iota(I32, (L, L), 0)
    jj = lax.broadcasted_iota(I32, (L, L), 1)
    mask = (jj >= ii) if rev else (jj <= ii)
    gk = gl_ref[0]
    g = _mm(mask.astype(F32), gk, HI)
    ri = L - 1 - L // 2 if rev else L // 2
    ei = 0 if rev else L - 1
    gref = g[ri:ri + 1, :]
    gend = g[ei:ei + 1, :]
    qf = q_ref[0].astype(F32)
    kf = k_ref[0].astype(F32)
    qg = (qf * jnp.exp(g - gref)).astype(BF16)
    kg = (kf * jnp.exp(gref - g)).astype(BF16)
    qe = (qf * jnp.exp(g)).astype(BF16)
    ku = (kf * jnp.exp(gend - g)).astype(BF16)
    dec = jnp.exp(gend)
    v = v_ref[0]
    for hh in range(GLA_H):
        ks = slice(hh * GLA_HK, (hh + 1) * GLA_HK)
        vs = slice(hh * GLA_HV, (hh + 1) * GLA_HV)
        att = jnp.where(mask, _nt(qg[:, ks], kg[:, ks]), 0.0)
        vh = v[:, vs]
        st = st_ref[d, hh]
        o = _mm(att.astype(BF16), vh) + _nt(qe[:, ks], st.astype(BF16))
        o_ref[0, :, vs] = o.astype(BF16)
        ut = _mm(vh.astype(F32).T.astype(BF16), ku[:, ks])
        st_ref[d, hh] = st * dec[:, ks] + ut


def _gla_scan_kernel(*refs, has_s0, want_sf, nchunk):
    refs = list(refs)
    fwd, bwd = refs[:4], refs[4:8]
    pos = 8
    s0_ref = None
    if has_s0:
        s0_ref = refs[pos]
        pos += 1
    of_ref, ob_ref = refs[pos], refs[pos + 1]
    pos += 2
    sf_ref = None
    if want_sf:
        sf_ref = refs[pos]
        pos += 1
    st_ref = refs[pos]
    c = pl.program_id(1)

    @pl.when(c == 0)
    def _init():
        for d in range(2):
            for hh in range(GLA_H):
                if has_s0:
                    st_ref[d, hh] = s0_ref[0, d, hh].T
                else:
                    st_ref[d, hh] = jnp.zeros((GLA_HV, GLA_HK), F32)

    _gla_dir_step(*fwd, of_ref, st_ref, 0, False)
    _gla_dir_step(*bwd, ob_ref, st_ref, 1, True)

    if want_sf:
        @pl.when(c == nchunk - 1)
        def _fin():
            for d in range(2):
                for hh in range(GLA_H):
                    sf_ref[0, d, hh] = st_ref[d, hh].T


def _gla_scan(q, k, v, gl, s0, *, nb, nchunk, want_sf):
    has_s0 = s0 is not None
    fidx = lambda b, c: b * nchunk + c
    bidx = lambda b, c: b * nchunk + (nchunk - 1 - c)
    blk = lambda w, f: pl.BlockSpec((1, GLA_L, w), lambda b, c: (f(b, c), 0, 0))
    in_specs = [blk(GLA_DK, fidx), blk(GLA_DK, fidx), blk(GLA_DV, fidx),
                pl.BlockSpec((1, GLA_L, GLA_DK), lambda b, c: (fidx(b, c), 0, 0)),
                blk(GLA_DK, bidx), blk(GLA_DK, bidx), blk(GLA_DV, bidx),
                pl.BlockSpec((1, GLA_L, GLA_DK), lambda b, c: (bidx(b, c), 0, 1))]
    args = [t.reshape(nb * nchunk, GLA_L, t.shape[-1]) for t in (q, k, v, gl)] * 2
    sshape = (1, 2, GLA_H, GLA_HK, GLA_HV)
    if has_s0:
        in_specs.append(pl.BlockSpec(sshape, lambda b, c: (b, 0, 0, 0, 0)))
        args.append(s0)
    out_specs = [blk(GLA_DV, fidx), blk(GLA_DV, bidx)]
    out_shape = [jax.ShapeDtypeStruct((nb * nchunk, GLA_L, GLA_DV), BF16)] * 2
    if want_sf:
        out_specs.append(pl.BlockSpec(sshape, lambda b, c: (b, 0, 0, 0, 0)))
        out_shape.append(jax.ShapeDtypeStruct((nb,) + sshape[1:], F32))
    return pl.pallas_call(
        functools.partial(_gla_scan_kernel, has_s0=has_s0, want_sf=want_sf, nchunk=nchunk),
        grid=(nb, nchunk),
        in_specs=in_specs, out_specs=out_specs, out_shape=out_shape,
        scratch_shapes=[pltpu.VMEM((2, GLA_H, GLA_HV, GLA_HK), F32)],
        compiler_params=_cparams("parallel", "arbitrary"),
        name="gla_scan_" + ("s" if has_s0 else "p"),
    )(*args)


def _gla_out_kernel(of_ref, ob_ref, r_ref, ng_ref, wo_ref, xres_ref,
                    gate_ref, g2_ref, sc2_ref, sh2_ref, rwT_ref, rb_ref, *rest, colmajor):
    if colmajor:
        (_, _, _, _, xn_ref, h2_ref, idx_ref, gcol_ref, so_ref, sr_ref) = rest
        nc = GLA_DV // 128
        for col in range(GRID_W):
            osum = of_ref[0, col].astype(F32) + ob_ref[0, col].astype(F32)
            rcol = r_ref[0, col].astype(F32)
            for jc in range(nc):
                so_ref[jc, pl.ds(col, 8, stride=GRID_W), :] = osum[:, jc * 128:(jc + 1) * 128]
                sr_ref[jc, pl.ds(col, 8, stride=GRID_W), :] = rcol[:, jc * 128:(jc + 1) * 128]
        o = jnp.concatenate([so_ref[jc] for jc in range(nc)], axis=1)
        r = jnp.concatenate([sr_ref[jc] for jc in range(nc)], axis=1)
    else:
        xn_ref, h2_ref, idx_ref, gcol_ref = rest
        o = of_ref[...].astype(F32) + ob_ref[...].astype(F32)
        r = r_ref[...].astype(F32)
    parts = []
    for hh in range(GLA_H):
        oh = o[:, hh * GLA_HV:(hh + 1) * GLA_HV]
        parts.append(oh * lax.rsqrt(jnp.mean(oh * oh, axis=-1, keepdims=True) + EPS) * ng_ref[...])
    y = jnp.concatenate(parts, axis=1) * _silu(r)
    xn = xres_ref[...] + gate_ref[0] * _mm(y.astype(BF16), wo_ref[...])
    xn_ref[...] = xn
    _router_epilogue(xn, g2_ref[...], sc2_ref[0], sh2_ref[0], rwT_ref[...], rb_ref[...],
                     h2_ref, idx_ref, gcol_ref)


def _gla_out(of, ob, r, norm_g, wo, x, router_args, prev, *, colmajor):
    tm = 8 * GRID_W if colmajor else TM
    off = NP // tm if colmajor else 0
    ntile = (NS if colmajor else NP) // tm
    full = lambda t: (0, 0)
    if colmajor:
        blk = lambda: pl.BlockSpec((1, GRID_W, 8, GLA_DV), lambda t: (t // 8, 0, t % 8, 0))
        scratch = [pltpu.VMEM((GLA_DV // 128, tm, 128), F32), pltpu.VMEM((GLA_DV // 128, tm, 128), F32)]
    else:
        blk = lambda: pl.BlockSpec((tm, GLA_DV), lambda t: (t, 0))
        scratch = []
    in_specs = ([blk(), blk(), blk(), pl.BlockSpec((1, GLA_HV), full), pl.BlockSpec((GLA_DV, D), full),
                 pl.BlockSpec((tm, D), lambda t: (t + off, 0))] + _router_in_specs(tm, off))
    args = [of, ob, r, norm_g, wo, x, *router_args]
    aliases = {}
    if prev is not None:
        for i, p in enumerate(prev):
            in_specs.append(pl.BlockSpec(memory_space=pl.ANY))
            args.append(p)
            aliases[len(args) - 1] = i
    return pl.pallas_call(
        functools.partial(_gla_out_kernel, colmajor=colmajor),
        grid=(ntile,),
        in_specs=in_specs, out_specs=_router_out_specs(tm, off), out_shape=_ROUTER_OUT_SHAPES,
        scratch_shapes=scratch, input_output_aliases=aliases,
        compiler_params=_cparams("parallel"),
        name="gla_out_" + ("s" if colmajor else "p"),
    )(*args)


def _sc_kernel(x_ref, sh_ref, sc_ref, g_ref, wb_ref, wc_ref, wu_ref, cw_ref, wo_ref,
               gate_ref, g2_ref, sc2_ref, sh2_ref, rwT_ref, rb_ref,
               xn_ref, h2_ref, idx_ref, gcol_ref, y_scr):
    t = pl.program_id(0)
    x = x_ref[...]
    hb = _modnorm(x, g_ref[...], sc_ref[0], sh_ref[0]).astype(BF16)
    segm1 = jnp.where(t < NP // TM, SEQ_P - 1, GRID_W - 1)
    cn = 512
    for j in range(0, D, cn):
        js = slice(j, j + cn)
        gcu = _mm(hb, wc_ref[:, js]) * _mm(hb, wu_ref[:, js])
        y_scr[:, js] = (_mm(hb, wb_ref[:, js]) * _seg_conv(gcu, cw_ref[:, js], 3, segm1)).astype(BF16)
    xn = x + gate_ref[0] * _mm(y_scr[...], wo_ref[...])
    xn_ref[...] = xn
    _router_epilogue(xn, g2_ref[...], sc2_ref[0], sh2_ref[0], rwT_ref[...], rb_ref[...],
                     h2_ref, idx_ref, gcol_ref)


def _shortconv(x, shift, scale, g, wb, wc, wu, conv_w, wo, router_args):
    row = lambda t: (_cond_row(t, TM), 0, 0)
    full = lambda t: (0, 0)
    return pl.pallas_call(
        _sc_kernel,
        grid=(T // TM,),
        in_specs=[pl.BlockSpec((TM, D), lambda t: (t, 0)),
                  pl.BlockSpec((1, 1, D), row), pl.BlockSpec((1, 1, D), row),
                  pl.BlockSpec((1, D), full),
                  pl.BlockSpec((D, D), full), pl.BlockSpec((D, D), full), pl.BlockSpec((D, D), full),
                  pl.BlockSpec((3, D), full), pl.BlockSpec((D, D), full)] + _router_in_specs(TM, 0),
        out_specs=_router_out_specs(TM, 0), out_shape=_ROUTER_OUT_SHAPES,
        scratch_shapes=[pltpu.VMEM((TM, D), BF16)],
        compiler_params=_cparams("parallel"),
        name="shortconv",
    )(x, shift, scale, g, wb, wc, wu, conv_w, wo, *router_args)


def _to_rows(tile_ref, m):
    return jnp.concatenate([tile_ref[pl.ds(s, m, stride=SUB), :] for s in range(SUB)], axis=1)


def _store_tiles(tile_ref, val, m):
    for s in range(SUB):
        tile_ref[pl.ds(s, m, stride=SUB), :] = val[:, s * 128:(s + 1) * 128]


def _moe_kernel(n_ref, ie_ref, chg_ref,
                tok0_ref, tok1_ref, tokn_ref, dstp_ref, dstc_ref,
                h_hbm, wgu_ref, bgu_ref, wd_ref, bd_ref, y_hbm,
                xbuf, obuf, act_scr, wgu_bf, wd_bf, gsem, ssem):
    i = pl.program_id(0)
    n = n_ref[0]
    slot = lax.rem(i, NSLOT)
    nslot = lax.rem(i + 2, NSLOT)

    def gather(idx_ref, r, s):
        src = h_hbm.at[pl.ds(pl.multiple_of(idx_ref[0, 0, r], SUB), SUB), :]
        return pltpu.make_async_copy(src, xbuf.at[s, pl.ds(pl.multiple_of(r * SUB, SUB), SUB), :], gsem.at[s])

    def scatter(idx_ref, r, s):
        dst = y_hbm.at[pl.ds(pl.multiple_of(idx_ref[0, 0, r], SUB), SUB), :]
        return pltpu.make_async_copy(obuf.at[s, pl.ds(pl.multiple_of(r * SUB, SUB), SUB), :], dst, ssem.at[s])

    def wait_gather(s):
        pltpu.make_async_copy(h_hbm.at[pl.ds(0, BM * SUB), :], xbuf.at[s], gsem.at[s]).wait()

    def wait_scatter(s):
        pltpu.make_async_copy(obuf.at[s], y_hbm.at[pl.ds(0, BM * SUB), :], ssem.at[s]).wait()

    @pl.when((i < n) & (chg_ref[i] == 1))
    def _cast_weights():
        wgu_bf[...] = wgu_ref[0, 0].astype(BF16)
        wd_bf[...] = wd_ref[0, 0].astype(BF16)

    @pl.when(i == 0)
    def _prologue():
        obuf[NSLOT - 1] = jnp.zeros((BM * SUB, 128), F32)

        def body(r, c):
            gather(tok0_ref, r, 0).start()
            gather(tok1_ref, r, 1).start()
            return c
        lax.fori_loop(0, BM, body, 0)

    @pl.when((i >= 2) & (i < n))
    def _free_obuf():
        wait_scatter(slot)

    @pl.when(i < n)
    def _main():
        wait_gather(slot)
        xb = _to_rows(xbuf.at[slot], BM).astype(BF16)
        half = DFF // 2
        per = BM // 4
        for q in range(4):
            for r in range(q * per, (q + 1) * per):
                gather(tokn_ref, r, nslot).start()
                scatter(dstp_ref, r, nslot).start()
            if q < 2:
                cs = slice(q * half, (q + 1) * half)
                us = slice(DFF + q * half, DFF + (q + 1) * half)
                gg = jnp.minimum(_mm(xb, wgu_bf[:, cs]) + bgu_ref[0, 0, :, cs], LIMIT)
                uu = jnp.clip(_mm(xb, wgu_bf[:, us]) + bgu_ref[0, 0, :, us], -LIMIT, LIMIT)
                act_scr[:, cs] = (gg * _sigmoid(ALPHA * gg) * (uu + 1.0)).astype(BF16)
            else:
                cs = slice((q - 2) * half, (q - 1) * half)
                yo = _mm(act_scr[...], wd_bf[:, cs]) + bd_ref[0, 0, :, cs]
                for s in range(SUB // 2):
                    sg = (q - 2) * (SUB // 2) + s
                    obuf[slot, pl.ds(sg, BM, stride=SUB), :] = yo[:, s * 128:(s + 1) * 128]

    @pl.when(i == n - 1)
    def _drain():
        pslot = lax.rem(i + 1, NSLOT)
        wait_scatter(pslot)
        wait_scatter(nslot)

        def body(r, c):
            scatter(dstc_ref, r, slot).start()
            return c
        lax.fori_loop(0, BM, body, 0)
        wait_scatter(slot)
        wait_gather(pslot)
        wait_gather(nslot)


def _moe_experts(plan, h2t, w_gu, b_gu, w_down, b_down, layer):
    n, ie, chg, tok8, dst8 = plan
    smem = lambda f: pl.BlockSpec((1, 1, BM), f, memory_space=pltpu.SMEM)
    wspec = lambda shape: pl.BlockSpec(shape, lambda i, n, ie, chg: (layer, ie[i], 0, 0))
    grid_spec = pltpu.PrefetchScalarGridSpec(
        num_scalar_prefetch=3,
        grid=(NB_TOT,),
        in_specs=[smem(lambda i, *_: (0, 0, 0)),
                  smem(lambda i, *_: (1, 0, 0)),
                  smem(lambda i, *_: (i + 2, 0, 0)),
                  smem(lambda i, *_: (i, 0, 0)),
                  smem(lambda i, *_: (i + 1, 0, 0)),
                  pl.BlockSpec(memory_space=pl.ANY),
                  wspec((1, 1, D, 2 * DFF)), wspec((1, 1, 1, 2 * DFF)),
                  wspec((1, 1, DFF, D)), wspec((1, 1, 1, D))],
        out_specs=pl.BlockSpec(memory_space=pl.ANY),
        scratch_shapes=[pltpu.VMEM((NSLOT, BM * SUB, 128), F32), pltpu.VMEM((NSLOT, BM * SUB, 128), F32),
                        pltpu.VMEM((BM, DFF), BF16),
                        pltpu.VMEM((D, 2 * DFF), BF16), pltpu.VMEM((DFF, D), BF16),
                        pltpu.SemaphoreType.DMA((NSLOT,)), pltpu.SemaphoreType.DMA((NSLOT,))],
    )
    return pl.pallas_call(
        _moe_kernel,
        grid_spec=grid_spec,
        out_shape=jax.ShapeDtypeStruct((Y_ROWS * SUB, 128), F32),
        compiler_params=_cparams("arbitrary"),
        name="moe_experts",
    )(n, ie, chg, tok8, tok8, tok8, dst8, dst8, h2t, w_gu, b_gu.reshape(DEPTH, NE, 1, 2 * DFF),
      w_down, b_down.reshape(DEPTH, NE, 1, D))


def _moe_plan(idx_t):
    e_flat = idx_t.reshape(NAS)
    abits = (NAS - 1).bit_length()
    order = jnp.sort((e_flat << abits) | jnp.arange(NAS, dtype=I32)) & ((1 << abits) - 1)
    counts = jnp.sum((e_flat[None, :] == jnp.arange(NE, dtype=I32)[:, None]).astype(I32), axis=1)
    nblk_e = (counts + BM - 1) // BM
    bend = jnp.cumsum(nblk_e).astype(I32)
    bstart = bend - nblk_e
    off_end = jnp.cumsum(counts).astype(I32)
    off = off_end - counts
    n = bend[-1]
    blk = jnp.arange(NB_TOT + 2, dtype=I32)
    be = jnp.minimum(jnp.sum((blk[:, None] >= bend[None, :]).astype(I32), axis=1), NE - 1)
    real = blk < n
    r = jnp.arange(BM, dtype=I32)[None, :]
    p = (off[be] + (blk - bstart[be]) * BM)[:, None] + r
    valid = real[:, None] & (p < off_end[be][:, None])
    a = order[jnp.clip(p, 0, NAS - 1)]
    tok8 = jnp.where(valid, (a % T) * SUB, 0)
    pad8 = (NAS + (blk % 2)[:, None] * BM + r) * SUB
    dst8 = jnp.where(valid, a * SUB, pad8)
    dst8 = jnp.concatenate([(NAS + BM + r) * SUB, dst8[:NB_TOT]], axis=0)
    ie = jnp.where(real, be, be[jnp.maximum(n - 1, 0)])[:NB_TOT]
    chg = (ie != jnp.concatenate([jnp.full((1,), -1, I32), ie[:-1]])).astype(I32)
    return (n.reshape(1), ie, chg, tok8.reshape(NB_TOT + 2, 1, BM), dst8.reshape(NB_TOT + 1, 1, BM))


def _combine_kernel(x_ref, y0_ref, y1_ref, y2_ref, y3_ref, gcol_ref, gate_ref, *rest, final):
    if final:
        fg_ref, o_ref = rest
    else:
        (o_ref,) = rest
    gc = gcol_ref[...]
    m = x_ref.shape[0]
    acc = _to_rows(y0_ref, m) * gc[:, 0:1]
    for k, y_ref in enumerate((y1_ref, y2_ref, y3_ref), start=1):
        acc = acc + _to_rows(y_ref, m) * gc[:, k:k + 1]
    xn = x_ref[...] + gate_ref[0] * acc
    if final:
        xn = xn * lax.rsqrt(jnp.mean(xn * xn, axis=-1, keepdims=True) + EPS) * fg_ref[...]
    o_ref[...] = xn


def _combine(x, y4, gcol, gate2, final_g):
    final = final_g is not None
    nt = T // TM
    yspec = lambda k: pl.BlockSpec((TM * SUB, 128), lambda t: (k * nt + t, 0))
    in_specs = [pl.BlockSpec((TM, D), lambda t: (t, 0)),
                yspec(0), yspec(1), yspec(2), yspec(3),
                pl.BlockSpec((TM, 128), lambda t: (t, 0)),
                pl.BlockSpec((1, 1, D), lambda t: (_cond_row(t, TM), 0, 0))]
    args = [x, y4, y4, y4, y4, gcol, gate2]
    if final:
        in_specs.append(pl.BlockSpec((1, D), lambda t: (0, 0)))
        args.append(final_g)
    return pl.pallas_call(
        functools.partial(_combine_kernel, final=final),
        grid=(nt,),
        in_specs=in_specs,
        out_specs=pl.BlockSpec((TM, D), lambda t: (t, 0)),
        out_shape=jax.ShapeDtypeStruct((T, D), F32),
        compiler_params=_cparams("parallel"),
        name="moe_combine",
    )(*args)


def kernel(x_prompt, x_sample, state_ssd, state_gla, c, c_ctx, w_mod, b_mod, norm1_g, norm2_g, ssd_w_in, ssd_conv_w, ssd_conv_b, ssd_dt_bias, ssd_a_log, ssd_d, ssd_norm_g, ssd_w_out, gla_w_in, gla_w_gate2, gla_b_gate2, gla_norm_g, gla_w_out, sc_w_in, sc_conv_w, sc_w_out, router_w, router_b, moe_w_gu, moe_b_gu, moe_w_down, moe_b_down, final_norm_g):
    x = jnp.concatenate([x_prompt.reshape(NP, D), x_sample.reshape(NS, D)], axis=0)
    cond8 = jnp.concatenate([c_ctx[None, :], c, jnp.zeros((8 - 1 - NB_S, D), F32)], axis=0)
    mod = _modulation(cond8, w_mod, b_mod)

    new_ssd, new_gla = [], []
    for i in range(DEPTH):
        m6 = [mod[i, :, j * D:(j + 1) * D].reshape(8, 1, D) for j in range(6)]
        shift1, scale1, gate1, shift2, scale2, gate2 = m6
        n1 = norm1_g[i].reshape(1, D)
        router_args = (gate1, norm2_g[i].reshape(1, D), scale2, shift2,
                       router_w[i].T, router_b[i].reshape(NE, 1))
        kind, j = i % 3, i // 3
        if kind == 0:
            w_in = ssd_w_in[j]
            wz = w_in[:, :DI].astype(BF16)
            wx = w_in[:, DI:DI + SSD_XBC].astype(BF16)
            wdt = w_in[:, DI + SSD_XBC:]
            z, xbc, dt, dtT = _ssd_in(x, shift1, scale1, n1, wz, wx, wdt, ssd_conv_w[j], ssd_conv_b[j],
                                      ssd_dt_bias[j].reshape(-1))
            a = -jnp.exp(ssd_a_log[j].astype(F32))
            ys, hfs = [], []
            for rev in (False, True):
                d = 1 if rev else 0
                yp, hf = _ssd_scan(xbc, dt, dtT, a, None, None, rev=rev, nb=NB_P,
                                   nchunk=SEQ_P // SSD_L, base=0, want_hf=True)
                h0 = state_ssd[:, j, d].reshape(NB_S, DI, SSD_N)
                (yd,) = _ssd_scan(xbc, dt, dtT, a, h0, yp, rev=rev, nb=NB_S,
                                  nchunk=SEQ_S // SSD_L, base=NP, want_hf=False)
                ys.append(yd)
                hfs.append(hf.reshape(NB_P, SSD_H, SSD_P, SSD_N))
            new_ssd.append(jnp.stack(hfs, axis=1))
            dskip_x = jnp.repeat(ssd_d[j].astype(F32), SSD_P).reshape(1, DI)
            x, h2, idx_t, gcol = _ssd_out(ys[0], ys[1], xbc, z, dskip_x, ssd_norm_g[j].reshape(1, DI),
                                          ssd_w_out[j].astype(BF16), x, router_args)
        elif kind == 1:
            w_in = gla_w_in[j]
            wq = w_in[:, :GLA_DK].astype(BF16)
            wk = w_in[:, GLA_DK:2 * GLA_DK].astype(BF16)
            wv = w_in[:, 2 * GLA_DK:2 * GLA_DK + GLA_DV].astype(BF16)
            wr = w_in[:, 2 * GLA_DK + GLA_DV:2 * GLA_DK + 2 * GLA_DV].astype(BF16)
            wg = w_in[:, 2 * GLA_DK + 2 * GLA_DV:]
            wo = gla_w_out[j].astype(BF16)
            ng = gla_norm_g[j].reshape(1, GLA_HV)
            prev = None
            for colmajor in (False, True):
                q, k, v, r, gl = _gla_in(x, shift1, scale1, n1, wq, wk, wv, wr, wg,
                                         gla_w_gate2[j], gla_b_gate2[j], colmajor=colmajor)
                nb = NB_S if colmajor else NB_P
                nchunk = (SEQ_S if colmajor else SEQ_P) // GLA_L
                res = _gla_scan(q, k, v, gl, state_gla[:, j] if colmajor else None,
                                nb=nb, nchunk=nchunk, want_sf=not colmajor)
                oshape = (NB_S, GRID_W, GRID_W, GLA_DV) if colmajor else (NP, GLA_DV)
                os_ = [res[0].reshape(oshape), res[1].reshape(oshape)]
                if not colmajor:
                    new_gla.append(res[2])
                prev = _gla_out(os_[0], os_[1], r, ng, wo, x, router_args, prev, colmajor=colmajor)
            x, h2, idx_t, gcol = prev
        else:
            w_in = sc_w_in[j]
            x, h2, idx_t, gcol = _shortconv(x, shift1, scale1, n1, w_in[:, :D].astype(BF16),
                                            w_in[:, D:2 * D].astype(BF16), w_in[:, 2 * D:].astype(BF16),
                                            sc_conv_w[j], sc_w_out[j].astype(BF16), router_args)
        y4 = _moe_experts(_moe_plan(idx_t), h2, moe_w_gu, moe_b_gu, moe_w_down, moe_b_down, i)
        x = _combine(x, y4, gcol, gate2, final_norm_g.reshape(1, D) if i == DEPTH - 1 else None)

    y_prompt = x[:NP].reshape(NB_P, SEQ_P, D)
    y_sample = x[NP:].reshape(NB_S, SEQ_S, D)
    new_state_ssd = jnp.stack(new_ssd, axis=1)
    new_state_gla = jnp.stack(new_gla, axis=1)
    return (y_prompt, y_sample, new_state_ssd, new_state_gla)
```

```python
import functools

import jax
import jax.numpy as jnp
from jax import lax
from jax.experimental import pallas as pl
from jax.experimental.pallas import tpu as pltpu

F32 = jnp.float32
BF16 = jnp.bfloat16
I32 = jnp.int32
HI = lax.Precision.HIGHEST

D = 1024
NB_P, SEQ_P = 32, 256
NB_S, SEQ_S = 4, 4096
GRID_W = 64
DEPTH = 4
EPS = 1e-6
NP = NB_P * SEQ_P
NS = NB_S * SEQ_S
T = NP + NS

DI = 2 * D
SSD_P = 64
SSD_H = DI // SSD_P
SSD_G = 4
SSD_N = 128
SSD_L = 128
SSD_XBC = DI + 2 * SSD_G * SSD_N
GLA_H = 4
GLA_DK = D // 2
GLA_DV = D
GLA_HK = GLA_DK // GLA_H
GLA_HV = GLA_DV // GLA_H
GLA_R = 16
GLA_NORM = 16.0
GLA_L = 64
NE = 32
TOPK = 4
DFF = D
ALPHA = 1.702
LIMIT = 7.0
NAS = T * TOPK

TM = 256
BM = 256
NB_TOT = NAS // BM + NE
Y_ROWS = NAS + 2 * BM
NSLOT = 3
SUB = 8
VMEM_LIMIT = 56 * 1024 * 1024


def _cparams(*sem):
    return pltpu.CompilerParams(dimension_semantics=sem, vmem_limit_bytes=VMEM_LIMIT)


def _nt(a, b, precision=None):
    return lax.dot_general(a, b, (((1,), (1,)), ((), ())), precision=precision,
                           preferred_element_type=F32)


def _mm(a, b, precision=None):
    return jnp.dot(a, b, precision=precision, preferred_element_type=F32)


def _split3(v):
    hi = v.astype(BF16)
    r1 = v - hi.astype(F32)
    mid = r1.astype(BF16)
    lo = (r1 - mid.astype(F32)).astype(BF16)
    return jnp.concatenate([hi, mid, lo], axis=1)


def _sigmoid(x):
    return 1.0 / (1.0 + jnp.exp(-x))


def _silu(x):
    return x * _sigmoid(x)


def _softplus(x):
    return jnp.maximum(x, 0.0) + jnp.log1p(jnp.exp(-jnp.abs(x)))


def _modnorm(x, g, scale, shift):
    y = x * lax.rsqrt(jnp.mean(x * x, axis=-1, keepdims=True) + EPS) * g
    return y * (1.0 + scale) + shift


def _cond_row(t, tm):
    npt = NP // tm
    return jnp.where(t < npt, 0, 1 + (t - npt) // (SEQ_S // tm))


def _seg_conv(acc, w, width, segm1):
    m = acc.shape[0]
    pos = lax.broadcasted_iota(I32, (m, 1), 0) & segm1
    half = width // 2
    out = acc * w[half:half + 1, :]
    for k in range(width):
        s = k - half
        if s == 0:
            continue
        rolled = pltpu.roll(acc, (m - s) % m, 0)
        valid = ((pos + s >= 0) & (pos + s <= segm1)).astype(F32)
        out = out + (rolled * valid) * w[k:k + 1, :]
    return out


def _mod_kernel(c_ref, w_ref, b_ref, o_ref):
    o_ref[0] = _mm(_silu(c_ref[...]), w_ref[0], HI) + b_ref[0]


def _modulation(cond8, w_mod, b_mod):
    nj = 6
    return pl.pallas_call(
        _mod_kernel,
        grid=(DEPTH, nj),
        in_specs=[pl.BlockSpec((8, D), lambda l, j: (0, 0)),
                  pl.BlockSpec((1, D, D), lambda l, j: (l, 0, j)),
                  pl.BlockSpec((1, 1, D), lambda l, j: (l, 0, j))],
        out_specs=pl.BlockSpec((1, 8, D), lambda l, j: (l, 0, j)),
        out_shape=jax.ShapeDtypeStruct((DEPTH, 8, 6 * D), F32),
        compiler_params=_cparams("parallel", "parallel"),
        name="modulation",
    )(cond8, w_mod, b_mod.reshape(DEPTH, 1, 6 * D))


def _router_epilogue(xn, g2, sc2, sh2, rwT, rb, h2_ref, idx_ref, gcol_ref):
    m = xn.shape[0]
    h2 = _modnorm(xn, g2, sc2, sh2)
    _store_tiles(h2_ref, h2, m)
    lg = _nt(rwT, h2, HI) + rb
    eidx = lax.broadcasted_iota(I32, (NE, m), 0)
    vals, idxs = [], []
    for _ in range(TOPK):
        mx = jnp.max(lg, axis=0, keepdims=True)
        sel = jnp.min(jnp.where(lg == mx, eidx, NE), axis=0, keepdims=True)
        vals.append(mx)
        idxs.append(sel)
        lg = jnp.where(eidx == sel, -jnp.inf, lg)
    ex = [jnp.exp(v - vals[0]) for v in vals]
    den = ex[0] + ex[1] + ex[2] + ex[3]
    row4 = lax.broadcasted_iota(I32, (TOPK, m), 0)
    idx = jnp.zeros((TOPK, m), I32)
    for k in range(TOPK):
        idx = jnp.where(row4 == k, idxs[k], idx)
    idx_ref[...] = idx
    row = lax.broadcasted_iota(I32, (128, m), 0)
    slab = jnp.zeros((128, m), F32)
    for k in range(TOPK):
        slab = jnp.where(row == k, ex[k] / den, slab)
    gcol_ref[...] = slab.T


_ROUTER_OUT_SHAPES = (jax.ShapeDtypeStruct((T, D), F32),
                      jax.ShapeDtypeStruct((T * SUB, 128), F32),
                      jax.ShapeDtypeStruct((TOPK, T), I32),
                      jax.ShapeDtypeStruct((T, 128), F32))


def _router_out_specs(tm, off):
    return [pl.BlockSpec((tm, D), lambda t: (t + off, 0)),
            pl.BlockSpec((tm * SUB, 128), lambda t: (t + off, 0)),
            pl.BlockSpec((TOPK, tm), lambda t: (0, t + off)),
            pl.BlockSpec((tm, 128), lambda t: (t + off, 0))]


def _router_in_specs(tm, off):
    row = lambda t: (_cond_row(t + off, tm), 0, 0)
    full = lambda t: (0, 0)
    return [pl.BlockSpec((1, 1, D), row),
            pl.BlockSpec((1, D), full),
            pl.BlockSpec((1, 1, D), row),
            pl.BlockSpec((1, 1, D), row),
            pl.BlockSpec((NE, D), full),
            pl.BlockSpec((NE, 1), full)]


def _ssd_in_kernel(x_ref, sh_ref, sc_ref, g_ref, wz_ref, wx_ref, wdt_ref, wdtT_ref, cw_ref, cb_ref,
                   dtb_ref, dtbT_ref, z_ref, xbc_ref, dt_ref, dtT_ref):
    t = pl.program_id(0)
    h = _modnorm(x_ref[...], g_ref[...], sc_ref[0], sh_ref[0])
    hb = h.astype(BF16)
    cn = 512
    for j in range(0, DI, cn):
        z_ref[:, j:j + cn] = _mm(hb, wz_ref[:, j:j + cn]).astype(BF16)
    segm1 = jnp.where(t < NP // TM, SEQ_P - 1, GRID_W - 1)
    for j in range(0, SSD_XBC, cn):
        acc = _mm(hb, wx_ref[:, j:j + cn])
        out = _seg_conv(acc, cw_ref[:, j:j + cn], 5, segm1) + cb_ref[:, j:j + cn]
        xbc_ref[:, j:j + cn] = _silu(out).astype(BF16)
    dt_ref[...] = _softplus(_mm(h, wdt_ref[...], HI) + dtb_ref[...])
    dtT_ref[...] = _softplus(_nt(wdtT_ref[...], h, HI) + dtbT_ref[...])


def _ssd_in(x, shift, scale, g, wz, wx, wdt, conv_w, conv_b, dt_bias):
    row = lambda t: (_cond_row(t, TM), 0, 0)
    full = lambda t: (0, 0)
    nh2 = 2 * SSD_H
    return pl.pallas_call(
        _ssd_in_kernel,
        grid=(T // TM,),
        in_specs=[pl.BlockSpec((TM, D), lambda t: (t, 0)),
                  pl.BlockSpec((1, 1, D), row), pl.BlockSpec((1, 1, D), row),
                  pl.BlockSpec((1, D), full),
                  pl.BlockSpec((D, DI), full), pl.BlockSpec((D, SSD_XBC), full),
                  pl.BlockSpec((D, nh2), full), pl.BlockSpec((nh2, D), full),
                  pl.BlockSpec((5, SSD_XBC), full), pl.BlockSpec((1, SSD_XBC), full),
                  pl.BlockSpec((1, nh2), full), pl.BlockSpec((nh2, 1), full)],
        out_specs=[pl.BlockSpec((TM, DI), lambda t: (t, 0)),
                   pl.BlockSpec((TM, SSD_XBC), lambda t: (t, 0)),
                   pl.BlockSpec((TM, nh2), lambda t: (t, 0)),
                   pl.BlockSpec((nh2, TM), lambda t: (0, t))],
        out_shape=[jax.ShapeDtypeStruct((T, DI), BF16), jax.ShapeDtypeStruct((T, SSD_XBC), BF16),
                   jax.ShapeDtypeStruct((T, nh2), F32), jax.ShapeDtypeStruct((nh2, T), F32)],
        compiler_params=_cparams("parallel"),
        name="ssd_in",
    )(x, shift, scale, g, wz, wx, wdt, wdt.T, conv_w, conv_b.reshape(1, -1),
      dt_bias.reshape(1, nh2), dt_bias.reshape(nh2, 1))


def _ssd_dir_step(xbc_ref, dt_ref, dtT_ref, a_ref, aT_ref, y_ref, s_ref, d, rev):
    L = SSD_L
    d0 = SSD_H if rev else 0
    ii = lax.broadcasted_iota(I32, (L, L), 0)
    jj = lax.broadcasted_iota(I32, (L, L), 1)
    mask = (jj >= ii) if rev else (jj <= ii)
    tri = mask.astype(F32)
    dt = dt_ref[:, d0:d0 + SSD_H]
    dtT = dtT_ref[...]
    acum = _mm(tri, dt * a_ref[...], HI)
    acumT = _nt(dtT * aT_ref[...], tri, HI)
    last = 0 if rev else L - 1
    tot = acum[last:last + 1, :]
    expand = ((lax.broadcasted_iota(I32, (3 * SSD_H, DI), 1) >> 6)
              == (lax.broadcasted_iota(I32, (3 * SSD_H, DI), 0) & (SSD_H - 1))).astype(BF16)
    eacum_x = _mm(_split3(jnp.exp(acum)), expand)
    wcol_x = _mm(_split3(jnp.exp(tot - acum) * dt), expand)
    etot_x = _mm(_split3(jnp.broadcast_to(jnp.exp(tot), (8, SSD_H))), expand)[0:1, :]
    lane = lax.broadcasted_iota(I32, (L, 128), 1)
    for g in range(SSD_G):
        gs = slice(g * 512, (g + 1) * 512)
        bg = xbc_ref[:, DI + g * SSD_N:DI + (g + 1) * SSD_N]
        cg = xbc_ref[:, DI + SSD_G * SSD_N + g * SSD_N:DI + SSD_G * SSD_N + (g + 1) * SSD_N]
        cb = _nt(cg, bg)
        sg = s_ref[d, :, gs]
        yint = _mm(cg, sg.astype(BF16))
        xg = xbc_ref[:, gs]
        for p in range(4):
            ms = []
            for hh in (g * 8 + 2 * p, g * 8 + 2 * p + 1):
                seg = acum[:, hh:hh + 1] - acumT[hh:hh + 1, :]
                dec = jnp.exp(jnp.where(mask, seg, -1e30))
                ms.append((cb * dec * dtT[hh:hh + 1, :]).astype(BF16))
            lhs = jnp.concatenate(ms, axis=1)
            xp = xg[:, p * 128:(p + 1) * 128]
            rhs = jnp.concatenate([jnp.where(lane < SSD_P, xp, jnp.zeros_like(xp)),
                                   jnp.where(lane >= SSD_P, xp, jnp.zeros_like(xp))], axis=0)
            cs = slice(g * 512 + p * 128, g * 512 + (p + 1) * 128)
            yp = _mm(lhs, rhs) + yint[:, p * 128:(p + 1) * 128] * eacum_x[:, cs]
            y_ref[:, cs] = yp.astype(BF16)
        xw = (xg.astype(F32) * wcol_x[:, gs]).astype(BF16)
        st = _mm(bg.astype(F32).T.astype(BF16), xw)
        s_ref[d, :, gs] = sg * etot_x[:, gs] + st


def _ssd_scan_kernel(*refs, has_h0, want_hf, nchunk):
    refs = list(refs)
    xf_ref, dtf_ref, dtTf_ref, xb_ref, dtb_ref, dtTb_ref, a0_ref, a0T_ref, a1_ref, a1T_ref = refs[:10]
    pos = 10
    h0_ref = None
    if has_h0:
        h0_ref = refs[pos]
        pos += 1
    yf_ref, yb_ref = refs[pos], refs[pos + 1]
    pos += 2
    hf_ref = None
    if want_hf:
        hf_ref = refs[pos]
        pos += 1
    s_ref = refs[pos]
    c = pl.program_id(1)

    @pl.when(c == 0)
    def _init():
        for d in range(2):
            if has_h0:
                s_ref[d] = h0_ref[0, d].T
            else:
                s_ref[d] = jnp.zeros((SSD_N, DI), F32)

    _ssd_dir_step(xf_ref, dtf_ref, dtTf_ref, a0_ref, a0T_ref, yf_ref, s_ref, 0, False)
    _ssd_dir_step(xb_ref, dtb_ref, dtTb_ref, a1_ref, a1T_ref, yb_ref, s_ref, 1, True)

    if want_hf:
        @pl.when(c == nchunk - 1)
        def _fin():
            for d in range(2):
                hf_ref[0, d] = s_ref[d].T


def _ssd_scan(xbc, dt, dtT, a, h0, y_prev, *, nb, nchunk, base, want_hf):
    has_h0 = h0 is not None
    cbase = base // SSD_L
    fidx = lambda b, c: cbase + b * nchunk + c
    bidx = lambda b, c: cbase + b * nchunk + (nchunk - 1 - c)
    const = lambda b, c: (0, 0)

    def stream(f, d):
        return [pl.BlockSpec((SSD_L, SSD_XBC), lambda b, c: (f(b, c), 0)),
                pl.BlockSpec((SSD_L, 2 * SSD_H), lambda b, c: (f(b, c), 0)),
                pl.BlockSpec((SSD_H, SSD_L), lambda b, c: (d, f(b, c)))]

    in_specs = (stream(fidx, 0) + stream(bidx, 1)
                + [pl.BlockSpec((1, SSD_H), const), pl.BlockSpec((SSD_H, 1), const)] * 2)
    args = [xbc, dt, dtT] * 2 + [a[0].reshape(1, SSD_H), a[0].reshape(SSD_H, 1),
                                 a[1].reshape(1, SSD_H), a[1].reshape(SSD_H, 1)]
    sshape = (1, 2, DI, SSD_N)
    if has_h0:
        in_specs.append(pl.BlockSpec(sshape, lambda b, c: (b, 0, 0, 0)))
        args.append(h0)
    out_specs = [pl.BlockSpec((SSD_L, DI), lambda b, c: (fidx(b, c), 0)),
                 pl.BlockSpec((SSD_L, DI), lambda b, c: (bidx(b, c), 0))]
    out_shape = [jax.ShapeDtypeStruct((T, DI), BF16)] * 2
    if want_hf:
        out_specs.append(pl.BlockSpec(sshape, lambda b, c: (b, 0, 0, 0)))
        out_shape.append(jax.ShapeDtypeStruct((nb,) + sshape[1:], F32))
    aliases = {}
    n_alias = 0
    if y_prev is not None:
        for i, yp in enumerate(y_prev):
            in_specs.append(pl.BlockSpec(memory_space=pl.ANY))
            args.append(yp)
            aliases[len(args) - 1] = i
        n_alias = len(y_prev)
    n_in = len(args)

    def body(*refs):
        refs = list(refs)
        refs = refs[:n_in - n_alias] + refs[n_in:]
        _ssd_scan_kernel(*refs, has_h0=has_h0, want_hf=want_hf, nchunk=nchunk)

    return pl.pallas_call(
        body,
        grid=(nb, nchunk),
        in_specs=in_specs, out_specs=out_specs, out_shape=out_shape,
        scratch_shapes=[pltpu.VMEM((2, SSD_N, DI), F32)],
        input_output_aliases=aliases,
        compiler_params=_cparams("parallel", "arbitrary"),
        name="ssd_scan_" + ("s" if has_h0 else "p"),
    )(*args)


def _ssd_out_kernel(yf_ref, yb_ref, x2_ref, z_ref, dsk_ref, ng_ref, wo_ref, xres_ref,
                    gate_ref, g2_ref, sc2_ref, sh2_ref, rwT_ref, rb_ref,
                    xn_ref, h2_ref, idx_ref, gcol_ref):
    y = yf_ref[...].astype(F32) + yb_ref[...].astype(F32) + dsk_ref[...] * x2_ref[...].astype(F32)
    y = y * _silu(z_ref[...].astype(F32))
    y = y * lax.rsqrt(jnp.mean(y * y, axis=-1, keepdims=True) + EPS) * ng_ref[...]
    xn = xres_ref[...] + gate_ref[0] * _mm(y.astype(BF16), wo_ref[...])
    xn_ref[...] = xn
    _router_epilogue(xn, g2_ref[...], sc2_ref[0], sh2_ref[0], rwT_ref[...], rb_ref[...],
                     h2_ref, idx_ref, gcol_ref)


def _ssd_out(yf, yb, xbc, z, dskip_x, norm_g, wo, x, router_args):
    full = lambda t: (0, 0)
    rowb = lambda t: (t, 0)
    return pl.pallas_call(
        _ssd_out_kernel,
        grid=(T // TM,),
        in_specs=[pl.BlockSpec((TM, DI), rowb), pl.BlockSpec((TM, DI), rowb),
                  pl.BlockSpec((TM, DI), rowb), pl.BlockSpec((TM, DI), rowb),
                  pl.BlockSpec((1, DI), full), pl.BlockSpec((1, DI), full),
                  pl.BlockSpec((DI, D), full), pl.BlockSpec((TM, D), rowb)] + _router_in_specs(TM, 0),
        out_specs=_router_out_specs(TM, 0),
        out_shape=_ROUTER_OUT_SHAPES,
        compiler_params=_cparams("parallel"),
        name="ssd_out",
    )(yf, yb, xbc, z, dskip_x, norm_g, wo, x, *router_args)


def _gla_in_kernel(x_ref, sh_ref, sc_ref, g_ref, wq_ref, wk_ref, wv_ref, wr_ref, wg_ref, w2_ref, b2_ref,
                   q_ref, k_ref, v_ref, r_ref, gl_ref, *scratch, colmajor):
    h = _modnorm(x_ref[...], g_ref[...], sc_ref[0], sh_ref[0])
    hb = h.astype(BF16)
    glow = _mm(h, wg_ref[...], HI)

    def gate(zdir, cs):
        pre = _mm(glow[:, zdir * GLA_R:(zdir + 1) * GLA_R], w2_ref[zdir, :, cs], HI) + b2_ref[zdir, :, cs]
        return (jnp.minimum(pre, 0.0) - jnp.log1p(jnp.exp(-jnp.abs(pre)))) / GLA_NORM

    prods = [(q_ref, GLA_DK, lambda cs: _mm(hb, wq_ref[:, cs]) * (GLA_HK ** -0.5)),
             (k_ref, GLA_DK, lambda cs: _mm(hb, wk_ref[:, cs])),
             (v_ref, GLA_DV, lambda cs: _mm(hb, wv_ref[:, cs])),
             (r_ref, GLA_DV, lambda cs: _mm(hb, wr_ref[:, cs])),
             (gl_ref, GLA_DK, lambda cs: gate(0, cs)),
             (gl_ref, GLA_DK, lambda cs: gate(1, cs))]
    cn = 256
    if not colmajor:
        for n_out, (o_ref, width, fn) in enumerate(prods):
            base = GLA_DK if n_out == 5 else 0
            for c in range(0, width, cn):
                o_ref[:, base + c:base + c + cn] = fn(slice(c, c + cn)).astype(o_ref.dtype)
    else:
        (scr,) = scratch
        for n_out, (o_ref, width, fn) in enumerate(prods):
            base = GLA_DK if n_out == 5 else 0
            for c in range(0, width, cn):
                val = fn(slice(c, c + cn))
                for jc in range(cn // 128):
                    scr[jc] = val[:, jc * 128:(jc + 1) * 128]
                for jc in range(cn // 128):
                    lo = base + c + jc * 128
                    for col in range(GRID_W):
                        o_ref[0, col, :, lo:lo + 128] = scr[jc, pl.ds(col, 8, stride=GRID_W), :].astype(o_ref.dtype)


def _gla_in(x, shift, scale, g, wq, wk, wv, wr, wg, w2, b2, *, colmajor):
    tm = 8 * GRID_W if colmajor else TM
    off = NP // tm if colmajor else 0
    ntile = (NS if colmajor else NP) // tm
    row = lambda t: (_cond_row(t + off, tm), 0, 0)
    full = lambda t: (0, 0)
    full3 = lambda t: (0, 0, 0)
    widths = [GLA_DK, GLA_DK, GLA_DV, GLA_DV, 2 * GLA_DK]
    dts = [BF16, BF16, BF16, BF16, F32]
    if colmajor:
        out_specs = [pl.BlockSpec((1, GRID_W, 8, w), lambda t: (t // 8, 0, t % 8, 0)) for w in widths]
        out_shape = [jax.ShapeDtypeStruct((NB_S, GRID_W, GRID_W, w), dt) for w, dt in zip(widths, dts)]
        scratch = [pltpu.VMEM((2, tm, 128), F32)]
    else:
        out_specs = [pl.BlockSpec((tm, w), lambda t: (t, 0)) for w in widths]
        out_shape = [jax.ShapeDtypeStruct((NP, w), dt) for w, dt in zip(widths, dts)]
        scratch = []
    return pl.pallas_call(
        functools.partial(_gla_in_kernel, colmajor=colmajor),
        grid=(ntile,),
        in_specs=[pl.BlockSpec((tm, D), lambda t: (t + off, 0)),
                  pl.BlockSpec((1, 1, D), row), pl.BlockSpec((1, 1, D), row),
                  pl.BlockSpec((1, D), full),
                  pl.BlockSpec((D, GLA_DK), full), pl.BlockSpec((D, GLA_DK), full),
                  pl.BlockSpec((D, GLA_DV), full), pl.BlockSpec((D, GLA_DV), full),
                  pl.BlockSpec((D, 2 * GLA_R), full),
                  pl.BlockSpec((2, GLA_R, GLA_DK), full3), pl.BlockSpec((2, 1, GLA_DK), full3)],
        out_specs=out_specs, out_shape=out_shape, scratch_shapes=scratch,
        compiler_params=_cparams("parallel"),
        name="gla_in_" + ("s" if colmajor else "p"),
    )(x, shift, scale, g, wq, wk, wv, wr, wg, w2, b2.reshape(2, 1, GLA_DK))


def _gla_dir_step(q_ref, k_ref, v_ref, gl_ref, o_ref, st_ref, d, rev):
    L = GLA_L
    ii = lax.broadcasted_iota(I32, (L, L), 0)
    jj = lax.broadcasted_iota(I32, (L, L), 1)
    mask = (jj >= ii) if rev else (jj <= ii)
    gk = gl_ref[0]
    g = _mm(mask.astype(F32), gk, HI)
    ri = L - 1 - L // 2 if rev else L // 2
    ei = 0 if rev else L - 1
    gref = g[ri:ri + 1, :]
    gend = g[ei:ei + 1, :]
    qf = q_ref[0].astype(F32)
    kf = k_ref[0].astype(F32)
    qg = (qf * jnp.exp(g - gref)).astype(BF16)
    kg = (kf * jnp.exp(gref - g)).astype(BF16)
    qe = (qf * jnp.exp(g)).astype(BF16)
    ku = (kf * jnp.exp(gend - g)).astype(BF16)
    dec = jnp.exp(gend)
    v = v_ref[0]
    for hh in range(GLA_H):
        ks = slice(hh * GLA_HK, (hh + 1) * GLA_HK)
        vs = slice(hh * GLA_HV, (hh + 1) * GLA_HV)
        att = jnp.where(mask, _nt(qg[:, ks], kg[:, ks]), 0.0)
        vh = v[:, vs]
        st = st_ref[d, hh]
        o = _mm(att.astype(BF16), vh) + _nt(qe[:, ks], st.astype(BF16))
        o_ref[0, :, vs] = o.astype(BF16)
        ut = _mm(vh.astype(F32).T.astype(BF16), ku[:, ks])
        st_ref[d, hh] = st * dec[:, ks] + ut


def _gla_scan_kernel(*refs, has_s0, want_sf, nchunk):
    refs = list(refs)
    fwd, bwd = refs[:4], refs[4:8]
    pos = 8
    s0_ref = None
    if has_s0:
        s0_ref = refs[pos]
        pos += 1
    of_ref, ob_ref = refs[pos], refs[pos + 1]
    pos += 2
    sf_ref = None
    if want_sf:
        sf_ref = refs[pos]
        pos += 1
    st_ref = refs[pos]
    c = pl.program_id(1)

    @pl.when(c == 0)
    def _init():
        for d in range(2):
            for hh in range(GLA_H):
                if has_s0:
                    st_ref[d, hh] = s0_ref[0, d, hh].T
                else:
                    st_ref[d, hh] = jnp.zeros((GLA_HV, GLA_HK), F32)

    _gla_dir_step(*fwd, of_ref, st_ref, 0, False)
    _gla_dir_step(*bwd, ob_ref, st_ref, 1, True)

    if want_sf:
        @pl.when(c == nchunk - 1)
        def _fin():
            for d in range(2):
                for hh in range(GLA_H):
                    sf_ref[0, d, hh] = st_ref[d, hh].T


def _gla_scan(q, k, v, gl, s0, *, nb, nchunk, want_sf):
    has_s0 = s0 is not None
    fidx = lambda b, c: b * nchunk + c
    bidx = lambda b, c: b * nchunk + (nchunk - 1 - c)
    blk = lambda w, f: pl.BlockSpec((1, GLA_L, w), lambda b, c: (f(b, c), 0, 0))
    in_specs = [blk(GLA_DK, fidx), blk(GLA_DK, fidx), blk(GLA_DV, fidx),
                pl.BlockSpec((1, GLA_L, GLA_DK), lambda b, c: (fidx(b, c), 0, 0)),
                blk(GLA_DK, bidx), blk(GLA_DK, bidx), blk(GLA_DV, bidx),
                pl.BlockSpec((1, GLA_L, GLA_DK), lambda b, c: (bidx(b, c), 0, 1))]
    args = [t.reshape(nb * nchunk, GLA_L, t.shape[-1]) for t in (q, k, v, gl)] * 2
    sshape = (1, 2, GLA_H, GLA_HK, GLA_HV)
    if has_s0:
        in_specs.append(pl.BlockSpec(sshape, lambda b, c: (b, 0, 0, 0, 0)))
        args.append(s0)
    out_specs = [blk(GLA_DV, fidx), blk(GLA_DV, bidx)]
    out_shape = [jax.ShapeDtypeStruct((nb * nchunk, GLA_L, GLA_DV), BF16)] * 2
    if want_sf:
        out_specs.append(pl.BlockSpec(sshape, lambda b, c: (b, 0, 0, 0, 0)))
        out_shape.append(jax.ShapeDtypeStruct((nb,) + sshape[1:], F32))
    return pl.pallas_call(
        functools.partial(_gla_scan_kernel, has_s0=has_s0, want_sf=want_sf, nchunk=nchunk),
        grid=(nb, nchunk),
        in_specs=in_specs, out_specs=out_specs, out_shape=out_shape,
        scratch_shapes=[pltpu.VMEM((2, GLA_H, GLA_HV, GLA_HK), F32)],
        compiler_params=_cparams("parallel", "arbitrary"),
        name="gla_scan_" + ("s" if has_s0 else "p"),
    )(*args)


def _gla_out_kernel(of_ref, ob_ref, r_ref, ng_ref, wo_ref, xres_ref,
                    gate_ref, g2_ref, sc2_ref, sh2_ref, rwT_ref, rb_ref, *rest, colmajor):
    if colmajor:
        (_, _, _, _, xn_ref, h2_ref, idx_ref, gcol_ref, so_ref, sr_ref) = rest
        nc = GLA_DV // 128
        for col in range(GRID_W):
            osum = of_ref[0, col].astype(F32) + ob_ref[0, col].astype(F32)
            rcol = r_ref[0, col].astype(F32)
            for jc in range(nc):
                so_ref[jc, pl.ds(col, 8, stride=GRID_W), :] = osum[:, jc * 128:(jc + 1) * 128]
                sr_ref[jc, pl.ds(col, 8, stride=GRID_W), :] = rcol[:, jc * 128:(jc + 1) * 128]
        o = jnp.concatenate([so_ref[jc] for jc in range(nc)], axis=1)
        r = jnp.concatenate([sr_ref[jc] for jc in range(nc)], axis=1)
    else:
        xn_ref, h2_ref, idx_ref, gcol_ref = rest
        o = of_ref[...].astype(F32) + ob_ref[...].astype(F32)
        r = r_ref[...].astype(F32)
    parts = []
    for hh in range(GLA_H):
        oh = o[:, hh * GLA_HV:(hh + 1) * GLA_HV]
        parts.append(oh * lax.rsqrt(jnp.mean(oh * oh, axis=-1, keepdims=True) + EPS) * ng_ref[...])
    y = jnp.concatenate(parts, axis=1) * _silu(r)
    xn = xres_ref[...] + gate_ref[0] * _mm(y.astype(BF16), wo_ref[...])
    xn_ref[...] = xn
    _router_epilogue(xn, g2_ref[...], sc2_ref[0], sh2_ref[0], rwT_ref[...], rb_ref[...],
                     h2_ref, idx_ref, gcol_ref)


def _gla_out(of, ob, r, norm_g, wo, x, router_args, prev, *, colmajor):
    tm = 8 * GRID_W if colmajor else TM
    off = NP // tm if colmajor else 0
    ntile = (NS if colmajor else NP) // tm
    full = lambda t: (0, 0)
    if colmajor:
        blk = lambda: pl.BlockSpec((1, GRID_W, 8, GLA_DV), lambda t: (t // 8, 0, t % 8, 0))
        scratch = [pltpu.VMEM((GLA_DV // 128, tm, 128), F32), pltpu.VMEM((GLA_DV // 128, tm, 128), F32)]
    else:
        blk = lambda: pl.BlockSpec((tm, GLA_DV), lambda t: (t, 0))
        scratch = []
    in_specs = ([blk(), blk(), blk(), pl.BlockSpec((1, GLA_HV), full), pl.BlockSpec((GLA_DV, D), full),
                 pl.BlockSpec((tm, D), lambda t: (t + off, 0))] + _router_in_specs(tm, off))
    args = [of, ob, r, norm_g, wo, x, *router_args]
    aliases = {}
    if prev is not None:
        for i, p in enumerate(prev):
            in_specs.append(pl.BlockSpec(memory_space=pl.ANY))
            args.append(p)
            aliases[len(args) - 1] = i
    return pl.pallas_call(
        functools.partial(_gla_out_kernel, colmajor=colmajor),
        grid=(ntile,),
        in_specs=in_specs, out_specs=_router_out_specs(tm, off), out_shape=_ROUTER_OUT_SHAPES,
        scratch_shapes=scratch, input_output_aliases=aliases,
        compiler_params=_cparams("parallel"),
        name="gla_out_" + ("s" if colmajor else "p"),
    )(*args)


def _sc_kernel(x_ref, sh_ref, sc_ref, g_ref, wb_ref, wc_ref, wu_ref, cw_ref, wo_ref,
               gate_ref, g2_ref, sc2_ref, sh2_ref, rwT_ref, rb_ref,
               xn_ref, h2_ref, idx_ref, gcol_ref, y_scr):
    t = pl.program_id(0)
    x = x_ref[...]
    hb = _modnorm(x, g_ref[...], sc_ref[0], sh_ref[0]).astype(BF16)
    segm1 = jnp.where(t < NP // TM, SEQ_P - 1, GRID_W - 1)
    cn = 512
    for j in range(0, D, cn):
        js = slice(j, j + cn)
        gcu = _mm(hb, wc_ref[:, js]) * _mm(hb, wu_ref[:, js])
        y_scr[:, js] = (_mm(hb, wb_ref[:, js]) * _seg_conv(gcu, cw_ref[:, js], 3, segm1)).astype(BF16)
    xn = x + gate_ref[0] * _mm(y_scr[...], wo_ref[...])
    xn_ref[...] = xn
    _router_epilogue(xn, g2_ref[...], sc2_ref[0], sh2_ref[0], rwT_ref[...], rb_ref[...],
                     h2_ref, idx_ref, gcol_ref)


def _shortconv(x, shift, scale, g, wb, wc, wu, conv_w, wo, router_args):
    row = lambda t: (_cond_row(t, TM), 0, 0)
    full = lambda t: (0, 0)
    return pl.pallas_call(
        _sc_kernel,
        grid=(T // TM,),
        in_specs=[pl.BlockSpec((TM, D), lambda t: (t, 0)),
                  pl.BlockSpec((1, 1, D), row), pl.BlockSpec((1, 1, D), row),
                  pl.BlockSpec((1, D), full),
                  pl.BlockSpec((D, D), full), pl.BlockSpec((D, D), full), pl.BlockSpec((D, D), full),
                  pl.BlockSpec((3, D), full), pl.BlockSpec((D, D), full)] + _router_in_specs(TM, 0),
        out_specs=_router_out_specs(TM, 0), out_shape=_ROUTER_OUT_SHAPES,
        scratch_shapes=[pltpu.VMEM((TM, D), BF16)],
        compiler_params=_cparams("parallel"),
        name="shortconv",
    )(x, shift, scale, g, wb, wc, wu, conv_w, wo, *router_args)


def _to_rows(tile_ref, m):
    return jnp.concatenate([tile_ref[pl.ds(s, m, stride=SUB), :] for s in range(SUB)], axis=1)


def _store_tiles(tile_ref, val, m):
    for s in range(SUB):
        tile_ref[pl.ds(s, m, stride=SUB), :] = val[:, s * 128:(s + 1) * 128]


def _moe_kernel(n_ref, ie_ref, chg_ref,
                tok0_ref, tok1_ref, tokn_ref, dstp_ref, dstc_ref,
                h_hbm, wgu_ref, bgu_ref, wd_ref, bd_ref, y_hbm,
                xbuf, obuf, act_scr, wgu_bf, wd_bf, gsem, ssem):
    i = pl.program_id(0)
    n = n_ref[0]
    slot = lax.rem(i, NSLOT)
    nslot = lax.rem(i + 2, NSLOT)

    def gather(idx_ref, r, s):
        src = h_hbm.at[pl.ds(pl.multiple_of(idx_ref[0, 0, r], SUB), SUB), :]
        return pltpu.make_async_copy(src, xbuf.at[s, pl.ds(pl.multiple_of(r * SUB, SUB), SUB), :], gsem.at[s])

    def scatter(idx_ref, r, s):
        dst = y_hbm.at[pl.ds(pl.multiple_of(idx_ref[0, 0, r], SUB), SUB), :]
        return pltpu.make_async_copy(obuf.at[s, pl.ds(pl.multiple_of(r * SUB, SUB), SUB), :], dst, ssem.at[s])

    def wait_gather(s):
        pltpu.make_async_copy(h_hbm.at[pl.ds(0, BM * SUB), :], xbuf.at[s], gsem.at[s]).wait()

    def wait_scatter(s):
        pltpu.make_async_copy(obuf.at[s], y_hbm.at[pl.ds(0, BM * SUB), :], ssem.at[s]).wait()

    @pl.when((i < n) & (chg_ref[i] == 1))
    def _cast_weights():
        wgu_bf[...] = wgu_ref[0, 0].astype(BF16)
        wd_bf[...] = wd_ref[0, 0].astype(BF16)

    @pl.when(i == 0)
    def _prologue():
        obuf[NSLOT - 1] = jnp.zeros((BM * SUB, 128), F32)

        def body(r, c):
            gather(tok0_ref, r, 0).start()
            gather(tok1_ref, r, 1).start()
            return c
        lax.fori_loop(0, BM, body, 0)

    @pl.when((i >= 2) & (i < n))
    def _free_obuf():
        wait_scatter(slot)

    @pl.when(i < n)
    def _main():
        wait_gather(slot)
        xb = _to_rows(xbuf.at[slot], BM).astype(BF16)
        half = DFF // 2
        per = BM // 4
        for q in range(4):
            for r in range(q * per, (q + 1) * per):
                gather(tokn_ref, r, nslot).start(priority=r % 2)
                scatter(dstp_ref, r, nslot).start(priority=(r + 1) % 2)
            if q < 2:
                cs = slice(q * half, (q + 1) * half)
                us = slice(DFF + q * half, DFF + (q + 1) * half)
                gg = jnp.minimum(_mm(xb, wgu_bf[:, cs]) + bgu_ref[0, 0, :, cs], LIMIT)
                uu = jnp.clip(_mm(xb, wgu_bf[:, us]) + bgu_ref[0, 0, :, us], -LIMIT, LIMIT)
                act_scr[:, cs] = (gg * _sigmoid(ALPHA * gg) * (uu + 1.0)).astype(BF16)
            else:
                cs = slice((q - 2) * half, (q - 1) * half)
                yo = _mm(act_scr[...], wd_bf[:, cs]) + bd_ref[0, 0, :, cs]
                for s in range(SUB // 2):
                    sg = (q - 2) * (SUB // 2) + s
                    obuf[slot, pl.ds(sg, BM, stride=SUB), :] = yo[:, s * 128:(s + 1) * 128]

    @pl.when(i == n - 1)
    def _drain():
        pslot = lax.rem(i + 1, NSLOT)
        wait_scatter(pslot)
        wait_scatter(nslot)

        def body(r, c):
            scatter(dstc_ref, r, slot).start()
            return c
        lax.fori_loop(0, BM, body, 0)
        wait_scatter(slot)
        wait_gather(pslot)
        wait_gather(nslot)


def _moe_experts(plan, h2t, w_gu, b_gu, w_down, b_down, layer):
    n, ie, chg, tok8, dst8 = plan
    smem = lambda f: pl.BlockSpec((1, 1, BM), f, memory_space=pltpu.SMEM)
    wspec = lambda shape: pl.BlockSpec(shape, lambda i, n, ie, chg: (layer, ie[i], 0, 0))
    grid_spec = pltpu.PrefetchScalarGridSpec(
        num_scalar_prefetch=3,
        grid=(NB_TOT,),
        in_specs=[smem(lambda i, *_: (0, 0, 0)),
                  smem(lambda i, *_: (1, 0, 0)),
                  smem(lambda i, *_: (i + 2, 0, 0)),
                  smem(lambda i, *_: (i, 0, 0)),
                  smem(lambda i, *_: (i + 1, 0, 0)),
                  pl.BlockSpec(memory_space=pl.ANY),
                  wspec((1, 1, D, 2 * DFF)), wspec((1, 1, 1, 2 * DFF)),
                  wspec((1, 1, DFF, D)), wspec((1, 1, 1, D))],
        out_specs=pl.BlockSpec(memory_space=pl.ANY),
        scratch_shapes=[pltpu.VMEM((NSLOT, BM * SUB, 128), F32), pltpu.VMEM((NSLOT, BM * SUB, 128), F32),
                        pltpu.VMEM((BM, DFF), BF16),
                        pltpu.VMEM((D, 2 * DFF), BF16), pltpu.VMEM((DFF, D), BF16),
                        pltpu.SemaphoreType.DMA((NSLOT,)), pltpu.SemaphoreType.DMA((NSLOT,))],
    )
    return pl.pallas_call(
        _moe_kernel,
        grid_spec=grid_spec,
        out_shape=jax.ShapeDtypeStruct((Y_ROWS * SUB, 128), F32),
        compiler_params=_cparams("arbitrary"),
        name="moe_experts",
    )(n, ie, chg, tok8, tok8, tok8, dst8, dst8, h2t, w_gu, b_gu.reshape(DEPTH, NE, 1, 2 * DFF),
      w_down, b_down.reshape(DEPTH, NE, 1, D))


def _moe_plan(idx_t):
    e_flat = idx_t.reshape(NAS)
    abits = (NAS - 1).bit_length()
    order = jnp.sort((e_flat << abits) | jnp.arange(NAS, dtype=I32)) & ((1 << abits) - 1)
    counts = jnp.sum((e_flat[None, :] == jnp.arange(NE, dtype=I32)[:, None]).astype(I32), axis=1)
    nblk_e = (counts + BM - 1) // BM
    bend = jnp.cumsum(nblk_e).astype(I32)
    bstart = bend - nblk_e
    off_end = jnp.cumsum(counts).astype(I32)
    off = off_end - counts
    n = bend[-1]
    blk = jnp.arange(NB_TOT + 2, dtype=I32)
    be = jnp.minimum(jnp.sum((blk[:, None] >= bend[None, :]).astype(I32), axis=1), NE - 1)
    real = blk < n
    r = jnp.arange(BM, dtype=I32)[None, :]
    p = (off[be] + (blk - bstart[be]) * BM)[:, None] + r
    valid = real[:, None] & (p < off_end[be][:, None])
    a = order[jnp.clip(p, 0, NAS - 1)]
    tok8 = jnp.where(valid, (a % T) * SUB, 0)
    pad8 = (NAS + (blk % 2)[:, None] * BM + r) * SUB
    dst8 = jnp.where(valid, a * SUB, pad8)
    dst8 = jnp.concatenate([(NAS + BM + r) * SUB, dst8[:NB_TOT]], axis=0)
    ie = jnp.where(real, be, be[jnp.maximum(n - 1, 0)])[:NB_TOT]
    chg = (ie != jnp.concatenate([jnp.full((1,), -1, I32), ie[:-1]])).astype(I32)
    return (n.reshape(1), ie, chg, tok8.reshape(NB_TOT + 2, 1, BM), dst8.reshape(NB_TOT + 1, 1, BM))


def _combine_kernel(x_ref, y0_ref, y1_ref, y2_ref, y3_ref, gcol_ref, gate_ref, *rest, final):
    if final:
        fg_ref, o_ref = rest
    else:
        (o_ref,) = rest
    gc = gcol_ref[...]
    m = x_ref.shape[0]
    acc = _to_rows(y0_ref, m) * gc[:, 0:1]
    for k, y_ref in enumerate((y1_ref, y2_ref, y3_ref), start=1):
        acc = acc + _to_rows(y_ref, m) * gc[:, k:k + 1]
    xn = x_ref[...] + gate_ref[0] * acc
    if final:
        xn = xn * lax.rsqrt(jnp.mean(xn * xn, axis=-1, keepdims=True) + EPS) * fg_ref[...]
    o_ref[...] = xn


def _combine(x, y4, gcol, gate2, final_g):
    final = final_g is not None
    nt = T // TM
    yspec = lambda k: pl.BlockSpec((TM * SUB, 128), lambda t: (k * nt + t, 0))
    in_specs = [pl.BlockSpec((TM, D), lambda t: (t, 0)),
                yspec(0), yspec(1), yspec(2), yspec(3),
                pl.BlockSpec((TM, 128), lambda t: (t, 0)),
                pl.BlockSpec((1, 1, D), lambda t: (_cond_row(t, TM), 0, 0))]
    args = [x, y4, y4, y4, y4, gcol, gate2]
    if final:
        in_specs.append(pl.BlockSpec((1, D), lambda t: (0, 0)))
        args.append(final_g)
    return pl.pallas_call(
        functools.partial(_combine_kernel, final=final),
        grid=(nt,),
        in_specs=in_specs,
        out_specs=pl.BlockSpec((TM, D), lambda t: (t, 0)),
        out_shape=jax.ShapeDtypeStruct((T, D), F32),
        compiler_params=_cparams("parallel"),
        name="moe_combine",
    )(*args)


def kernel(x_prompt, x_sample, state_ssd, state_gla, c, c_ctx, w_mod, b_mod, norm1_g, norm2_g, ssd_w_in, ssd_conv_w, ssd_conv_b, ssd_dt_bias, ssd_a_log, ssd_d, ssd_norm_g, ssd_w_out, gla_w_in, gla_w_gate2, gla_b_gate2, gla_norm_g, gla_w_out, sc_w_in, sc_conv_w, sc_w_out, router_w, router_b, moe_w_gu, moe_b_gu, moe_w_down, moe_b_down, final_norm_g):
    x = jnp.concatenate([x_prompt.reshape(NP, D), x_sample.reshape(NS, D)], axis=0)
    cond8 = jnp.concatenate([c_ctx[None, :], c, jnp.zeros((8 - 1 - NB_S, D), F32)], axis=0)
    mod = _modulation(cond8, w_mod, b_mod)

    new_ssd, new_gla = [], []
    for i in range(DEPTH):
        m6 = [mod[i, :, j * D:(j + 1) * D].reshape(8, 1, D) for j in range(6)]
        shift1, scale1, gate1, shift2, scale2, gate2 = m6
        n1 = norm1_g[i].reshape(1, D)
        router_args = (gate1, norm2_g[i].reshape(1, D), scale2, shift2,
                       router_w[i].T, router_b[i].reshape(NE, 1))
        kind, j = i % 3, i // 3
        if kind == 0:
            w_in = ssd_w_in[j]
            wz = w_in[:, :DI].astype(BF16)
            wx = w_in[:, DI:DI + SSD_XBC].astype(BF16)
            wdt = w_in[:, DI + SSD_XBC:]
            z, xbc, dt, dtT = _ssd_in(x, shift1, scale1, n1, wz, wx, wdt, ssd_conv_w[j], ssd_conv_b[j],
                                      ssd_dt_bias[j].reshape(-1))
            a = -jnp.exp(ssd_a_log[j].astype(F32))
            yfp, ybp, hf = _ssd_scan(xbc, dt, dtT, a, None, None, nb=NB_P, nchunk=SEQ_P // SSD_L,
                                     base=0, want_hf=True)
            ys = _ssd_scan(xbc, dt, dtT, a, state_ssd[:, j].reshape(NB_S, 2, DI, SSD_N), (yfp, ybp),
                           nb=NB_S, nchunk=SEQ_S // SSD_L, base=NP, want_hf=False)
            new_ssd.append(hf.reshape(NB_P, 2, SSD_H, SSD_P, SSD_N))
            dskip_x = jnp.repeat(ssd_d[j].astype(F32), SSD_P).reshape(1, DI)
            x, h2, idx_t, gcol = _ssd_out(ys[0], ys[1], xbc, z, dskip_x, ssd_norm_g[j].reshape(1, DI),
                                          ssd_w_out[j].astype(BF16), x, router_args)
        elif kind == 1:
            w_in = gla_w_in[j]
            wq = w_in[:, :GLA_DK].astype(BF16)
            wk = w_in[:, GLA_DK:2 * GLA_DK].astype(BF16)
            wv = w_in[:, 2 * GLA_DK:2 * GLA_DK + GLA_DV].astype(BF16)
            wr = w_in[:, 2 * GLA_DK + GLA_DV:2 * GLA_DK + 2 * GLA_DV].astype(BF16)
            wg = w_in[:, 2 * GLA_DK + 2 * GLA_DV:]
            wo = gla_w_out[j].astype(BF16)
            ng = gla_norm_g[j].reshape(1, GLA_HV)
            prev = None
            for colmajor in (False, True):
                q, k, v, r, gl = _gla_in(x, shift1, scale1, n1, wq, wk, wv, wr, wg,
                                         gla_w_gate2[j], gla_b_gate2[j], colmajor=colmajor)
                nb = NB_S if colmajor else NB_P
                nchunk = (SEQ_S if colmajor else SEQ_P) // GLA_L
                res = _gla_scan(q, k, v, gl, state_gla[:, j] if colmajor else None,
                                nb=nb, nchunk=nchunk, want_sf=not colmajor)
                oshape = (NB_S, GRID_W, GRID_W, GLA_DV) if colmajor else (NP, GLA_DV)
                os_ = [res[0].reshape(oshape), res[1].reshape(oshape)]
                if not colmajor:
                    new_gla.append(res[2])
                prev = _gla_out(os_[0], os_[1], r, ng, wo, x, router_args, prev, colmajor=colmajor)
            x, h2, idx_t, gcol = prev
        else:
            w_in = sc_w_in[j]
            x, h2, idx_t, gcol = _shortconv(x, shift1, scale1, n1, w_in[:, :D].astype(BF16),
                                            w_in[:, D:2 * D].astype(BF16), w_in[:, 2 * D:].astype(BF16),
                                            sc_conv_w[j], sc_w_out[j].astype(BF16), router_args)
        y4 = _moe_experts(_moe_plan(idx_t), h2, moe_w_gu, moe_b_gu, moe_w_down, moe_b_down, i)
        x = _combine(x, y4, gcol, gate2, final_norm_g.reshape(1, D) if i == DEPTH - 1 else None)

    y_prompt = x[:NP].reshape(NB_P, SEQ_P, D)
    y_sample = x[NP:].reshape(NB_S, SEQ_S, D)
    new_state_ssd = jnp.stack(new_ssd, axis=1)
    new_state_gla = jnp.stack(new_gla, axis=1)
    return (y_prompt, y_sample, new_state_ssd, new_state_gla)
```

```python
import functools

import jax
import jax.numpy as jnp
from jax import lax
from jax.experimental import pallas as pl
from jax.experimental.pallas import tpu as pltpu

F32 = jnp.float32
BF16 = jnp.bfloat16
I32 = jnp.int32
HI = lax.Precision.HIGHEST

D = 1024
NB_P, SEQ_P = 32, 256
NB_S, SEQ_S = 4, 4096
GRID_W = 64
DEPTH = 4
EPS = 1e-6
NP = NB_P * SEQ_P
NS = NB_S * SEQ_S
T = NP + NS

DI = 2 * D
SSD_P = 64
SSD_H = DI // SSD_P
SSD_G = 4
SSD_N = 128
SSD_L = 128
SSD_XBC = DI + 2 * SSD_G * SSD_N
GLA_H = 4
GLA_DK = D // 2
GLA_DV = D
GLA_HK = GLA_DK // GLA_H
GLA_HV = GLA_DV // GLA_H
GLA_R = 16
GLA_NORM = 16.0
GLA_L = 64
NE = 32
TOPK = 4
DFF = D
ALPHA = 1.702
LIMIT = 7.0
NAS = T * TOPK

TM = 256
BM = 512
NB_TOT = NAS // BM + NE
Y_ROWS = NAS + 2 * BM
NSLOT = 3
SUB = 8
VMEM_LIMIT = 56 * 1024 * 1024


def _cparams(*sem):
    return pltpu.CompilerParams(dimension_semantics=sem, vmem_limit_bytes=VMEM_LIMIT)


def _nt(a, b, precision=None):
    return lax.dot_general(a, b, (((1,), (1,)), ((), ())), precision=precision,
                           preferred_element_type=F32)


def _mm(a, b, precision=None):
    return jnp.dot(a, b, precision=precision, preferred_element_type=F32)


def _split3(v):
    hi = v.astype(BF16)
    r1 = v - hi.astype(F32)
    mid = r1.astype(BF16)
    lo = (r1 - mid.astype(F32)).astype(BF16)
    return jnp.concatenate([hi, mid, lo], axis=1)


def _sigmoid(x):
    return 1.0 / (1.0 + jnp.exp(-x))


def _silu(x):
    return x * _sigmoid(x)


def _softplus(x):
    return jnp.maximum(x, 0.0) + jnp.log1p(jnp.exp(-jnp.abs(x)))


def _modnorm(x, g, scale, shift):
    y = x * lax.rsqrt(jnp.mean(x * x, axis=-1, keepdims=True) + EPS) * g
    return y * (1.0 + scale) + shift


def _cond_row(t, tm):
    npt = NP // tm
    return jnp.where(t < npt, 0, 1 + (t - npt) // (SEQ_S // tm))


def _seg_conv(acc, w, width, segm1):
    m = acc.shape[0]
    pos = lax.broadcasted_iota(I32, (m, 1), 0) & segm1
    half = width // 2
    out = acc * w[half:half + 1, :]
    for k in range(width):
        s = k - half
        if s == 0:
            continue
        rolled = pltpu.roll(acc, (m - s) % m, 0)
        valid = ((pos + s >= 0) & (pos + s <= segm1)).astype(F32)
        out = out + (rolled * valid) * w[k:k + 1, :]
    return out


def _mod_kernel(c_ref, w_ref, b_ref, o_ref):
    o_ref[0] = _mm(_silu(c_ref[...]), w_ref[0], HI) + b_ref[0]


def _modulation(cond8, w_mod, b_mod):
    nj = 6
    return pl.pallas_call(
        _mod_kernel,
        grid=(DEPTH, nj),
        in_specs=[pl.BlockSpec((8, D), lambda l, j: (0, 0)),
                  pl.BlockSpec((1, D, D), lambda l, j: (l, 0, j)),
                  pl.BlockSpec((1, 1, D), lambda l, j: (l, 0, j))],
        out_specs=pl.BlockSpec((1, 8, D), lambda l, j: (l, 0, j)),
        out_shape=jax.ShapeDtypeStruct((DEPTH, 8, 6 * D), F32),
        compiler_params=_cparams("parallel", "parallel"),
        name="modulation",
    )(cond8, w_mod, b_mod.reshape(DEPTH, 1, 6 * D))


def _router_epilogue(xn, g2, sc2, sh2, rwT, rb, h2_ref, idx_ref, gcol_ref):
    m = xn.shape[0]
    h2 = _modnorm(xn, g2, sc2, sh2)
    _store_tiles(h2_ref, h2, m)
    lg = _nt(rwT, h2, HI) + rb
    eidx = lax.broadcasted_iota(I32, (NE, m), 0)
    vals, idxs = [], []
    for _ in range(TOPK):
        mx = jnp.max(lg, axis=0, keepdims=True)
        sel = jnp.min(jnp.where(lg == mx, eidx, NE), axis=0, keepdims=True)
        vals.append(mx)
        idxs.append(sel)
        lg = jnp.where(eidx == sel, -jnp.inf, lg)
    ex = [jnp.exp(v - vals[0]) for v in vals]
    den = ex[0] + ex[1] + ex[2] + ex[3]
    row4 = lax.broadcasted_iota(I32, (TOPK, m), 0)
    idx = jnp.zeros((TOPK, m), I32)
    for k in range(TOPK):
        idx = jnp.where(row4 == k, idxs[k], idx)
    idx_ref[...] = idx
    row = lax.broadcasted_iota(I32, (128, m), 0)
    slab = jnp.zeros((128, m), F32)
    for k in range(TOPK):
        slab = jnp.where(row == k, ex[k] / den, slab)
    gcol_ref[...] = slab.T


_ROUTER_OUT_SHAPES = (jax.ShapeDtypeStruct((T, D), F32),
                      jax.ShapeDtypeStruct((T * SUB, 128), F32),
                      jax.ShapeDtypeStruct((TOPK, T), I32),
                      jax.ShapeDtypeStruct((T, 128), F32))


def _router_out_specs(tm, off):
    return [pl.BlockSpec((tm, D), lambda t: (t + off, 0)),
            pl.BlockSpec((tm * SUB, 128), lambda t: (t + off, 0)),
            pl.BlockSpec((TOPK, tm), lambda t: (0, t + off)),
            pl.BlockSpec((tm, 128), lambda t: (t + off, 0))]


def _router_in_specs(tm, off):
    row = lambda t: (_cond_row(t + off, tm), 0, 0)
    full = lambda t: (0, 0)
    return [pl.BlockSpec((1, 1, D), row),
            pl.BlockSpec((1, D), full),
            pl.BlockSpec((1, 1, D), row),
            pl.BlockSpec((1, 1, D), row),
            pl.BlockSpec((NE, D), full),
            pl.BlockSpec((NE, 1), full)]


def _ssd_in_kernel(x_ref, sh_ref, sc_ref, g_ref, wz_ref, wx_ref, wdt_ref, wdtT_ref, cw_ref, cb_ref,
                   dtb_ref, dtbT_ref, z_ref, xbc_ref, dt_ref, dtT_ref):
    t = pl.program_id(0)
    h = _modnorm(x_ref[...], g_ref[...], sc_ref[0], sh_ref[0])
    hb = h.astype(BF16)
    cn = 512
    for j in range(0, DI, cn):
        z_ref[:, j:j + cn] = _mm(hb, wz_ref[:, j:j + cn]).astype(BF16)
    segm1 = jnp.where(t < NP // TM, SEQ_P - 1, GRID_W - 1)
    for j in range(0, SSD_XBC, cn):
        acc = _mm(hb, wx_ref[:, j:j + cn])
        out = _seg_conv(acc, cw_ref[:, j:j + cn], 5, segm1) + cb_ref[:, j:j + cn]
        xbc_ref[:, j:j + cn] = _silu(out).astype(BF16)
    dt_ref[...] = _softplus(_mm(h, wdt_ref[...], HI) + dtb_ref[...])
    dtT_ref[...] = _softplus(_nt(wdtT_ref[...], h, HI) + dtbT_ref[...])


def _ssd_in(x, shift, scale, g, wz, wx, wdt, conv_w, conv_b, dt_bias):
    row = lambda t: (_cond_row(t, TM), 0, 0)
    full = lambda t: (0, 0)
    nh2 = 2 * SSD_H
    return pl.pallas_call(
        _ssd_in_kernel,
        grid=(T // TM,),
        in_specs=[pl.BlockSpec((TM, D), lambda t: (t, 0)),
                  pl.BlockSpec((1, 1, D), row), pl.BlockSpec((1, 1, D), row),
                  pl.BlockSpec((1, D), full),
                  pl.BlockSpec((D, DI), full), pl.BlockSpec((D, SSD_XBC), full),
                  pl.BlockSpec((D, nh2), full), pl.BlockSpec((nh2, D), full),
                  pl.BlockSpec((5, SSD_XBC), full), pl.BlockSpec((1, SSD_XBC), full),
                  pl.BlockSpec((1, nh2), full), pl.BlockSpec((nh2, 1), full)],
        out_specs=[pl.BlockSpec((TM, DI), lambda t: (t, 0)),
                   pl.BlockSpec((TM, SSD_XBC), lambda t: (t, 0)),
                   pl.BlockSpec((TM, nh2), lambda t: (t, 0)),
                   pl.BlockSpec((nh2, TM), lambda t: (0, t))],
        out_shape=[jax.ShapeDtypeStruct((T, DI), BF16), jax.ShapeDtypeStruct((T, SSD_XBC), BF16),
                   jax.ShapeDtypeStruct((T, nh2), F32), jax.ShapeDtypeStruct((nh2, T), F32)],
        compiler_params=_cparams("parallel"),
        name="ssd_in",
    )(x, shift, scale, g, wz, wx, wdt, wdt.T, conv_w, conv_b.reshape(1, -1),
      dt_bias.reshape(1, nh2), dt_bias.reshape(nh2, 1))


def _ssd_dir_step(xbc_ref, dt_ref, dtT_ref, a_ref, aT_ref, y_ref, s_ref, d, rev):
    L = SSD_L
    d0 = SSD_H if rev else 0
    ii = lax.broadcasted_iota(I32, (L, L), 0)
    jj = lax.broadcasted_iota(I32, (L, L), 1)
    mask = (jj >= ii) if rev else (jj <= ii)
    tri = mask.astype(F32)
    dt = dt_ref[:, d0:d0 + SSD_H]
    dtT = dtT_ref[...]
    acum = _mm(tri, dt * a_ref[...], HI)
    acumT = _nt(dtT * aT_ref[...], tri, HI)
    last = 0 if rev else L - 1
    tot = acum[last:last + 1, :]
    expand = ((lax.broadcasted_iota(I32, (3 * SSD_H, DI), 1) >> 6)
              == (lax.broadcasted_iota(I32, (3 * SSD_H, DI), 0) & (SSD_H - 1))).astype(BF16)
    eacum_x = _mm(_split3(jnp.exp(acum)), expand)
    wcol_x = _mm(_split3(jnp.exp(tot - acum) * dt), expand)
    etot_x = _mm(_split3(jnp.broadcast_to(jnp.exp(tot), (8, SSD_H))), expand)[0:1, :]
    lane = lax.broadcasted_iota(I32, (L, 128), 1)
    for g in range(SSD_G):
        gs = slice(g * 512, (g + 1) * 512)
        bg = xbc_ref[:, DI + g * SSD_N:DI + (g + 1) * SSD_N]
        cg = xbc_ref[:, DI + SSD_G * SSD_N + g * SSD_N:DI + SSD_G * SSD_N + (g + 1) * SSD_N]
        cb = _nt(cg, bg)
        sg = s_ref[d, :, gs]
        yint = _mm(cg, sg.astype(BF16))
        xg = xbc_ref[:, gs]
        for p in range(4):
            ms = []
            for hh in (g * 8 + 2 * p, g * 8 + 2 * p + 1):
                seg = acum[:, hh:hh + 1] - acumT[hh:hh + 1, :]
                dec = jnp.exp(jnp.where(mask, seg, -1e30))
                ms.append((cb * dec * dtT[hh:hh + 1, :]).astype(BF16))
            lhs = jnp.concatenate(ms, axis=1)
            xp = xg[:, p * 128:(p + 1) * 128]
            rhs = jnp.concatenate([jnp.where(lane < SSD_P, xp, jnp.zeros_like(xp)),
                                   jnp.where(lane >= SSD_P, xp, jnp.zeros_like(xp))], axis=0)
            cs = slice(g * 512 + p * 128, g * 512 + (p + 1) * 128)
            yp = _mm(lhs, rhs) + yint[:, p * 128:(p + 1) * 128] * eacum_x[:, cs]
            y_ref[:, cs] = yp.astype(BF16)
        xw = (xg.astype(F32) * wcol_x[:, gs]).astype(BF16)
        st = _mm(bg.astype(F32).T.astype(BF16), xw)
        s_ref[d, :, gs] = sg * etot_x[:, gs] + st


def _ssd_scan_kernel(*refs, has_h0, want_hf, nchunk):
    refs = list(refs)
    xf_ref, dtf_ref, dtTf_ref, xb_ref, dtb_ref, dtTb_ref, a0_ref, a0T_ref, a1_ref, a1T_ref = refs[:10]
    pos = 10
    h0_ref = None
    if has_h0:
        h0_ref = refs[pos]
        pos += 1
    yf_ref, yb_ref = refs[pos], refs[pos + 1]
    pos += 2
    hf_ref = None
    if want_hf:
        hf_ref = refs[pos]
        pos += 1
    s_ref = refs[pos]
    c = pl.program_id(1)

    @pl.when(c == 0)
    def _init():
        for d in range(2):
            if has_h0:
                s_ref[d] = h0_ref[0, d].T
            else:
                s_ref[d] = jnp.zeros((SSD_N, DI), F32)

    _ssd_dir_step(xf_ref, dtf_ref, dtTf_ref, a0_ref, a0T_ref, yf_ref, s_ref, 0, False)
    _ssd_dir_step(xb_ref, dtb_ref, dtTb_ref, a1_ref, a1T_ref, yb_ref, s_ref, 1, True)

    if want_hf:
        @pl.when(c == nchunk - 1)
        def _fin():
            for d in range(2):
                hf_ref[0, d] = s_ref[d].T


def _ssd_scan(xbc, dt, dtT, a, h0, y_prev, *, nb, nchunk, base, want_hf):
    has_h0 = h0 is not None
    cbase = base // SSD_L
    fidx = lambda b, c: cbase + b * nchunk + c
    bidx = lambda b, c: cbase + b * nchunk + (nchunk - 1 - c)
    const = lambda b, c: (0, 0)

    def stream(f, d):
        return [pl.BlockSpec((SSD_L, SSD_XBC), lambda b, c: (f(b, c), 0)),
                pl.BlockSpec((SSD_L, 2 * SSD_H), lambda b, c: (f(b, c), 0)),
                pl.BlockSpec((SSD_H, SSD_L), lambda b, c: (d, f(b, c)))]

    in_specs = (stream(fidx, 0) + stream(bidx, 1)
                + [pl.BlockSpec((1, SSD_H), const), pl.BlockSpec((SSD_H, 1), const)] * 2)
    args = [xbc, dt, dtT] * 2 + [a[0].reshape(1, SSD_H), a[0].reshape(SSD_H, 1),
                                 a[1].reshape(1, SSD_H), a[1].reshape(SSD_H, 1)]
    sshape = (1, 2, DI, SSD_N)
    if has_h0:
        in_specs.append(pl.BlockSpec(sshape, lambda b, c: (b, 0, 0, 0)))
        args.append(h0)
    out_specs = [pl.BlockSpec((SSD_L, DI), lambda b, c: (fidx(b, c), 0)),
                 pl.BlockSpec((SSD_L, DI), lambda b, c: (bidx(b, c), 0))]
    out_shape = [jax.ShapeDtypeStruct((T, DI), BF16)] * 2
    if want_hf:
        out_specs.append(pl.BlockSpec(sshape, lambda b, c: (b, 0, 0, 0)))
        out_shape.append(jax.ShapeDtypeStruct((nb,) + sshape[1:], F32))
    aliases = {}
    n_alias = 0
    if y_prev is not None:
        for i, yp in enumerate(y_prev):
            in_specs.append(pl.BlockSpec(memory_space=pl.ANY))
            args.append(yp)
            aliases[len(args) - 1] = i
        n_alias = len(y_prev)
    n_in = len(args)

    def body(*refs):
        refs = list(refs)
        refs = refs[:n_in - n_alias] + refs[n_in:]
        _ssd_scan_kernel(*refs, has_h0=has_h0, want_hf=want_hf, nchunk=nchunk)

    return pl.pallas_call(
        body,
        grid=(nb, nchunk),
        in_specs=in_specs, out_specs=out_specs, out_shape=out_shape,
        scratch_shapes=[pltpu.VMEM((2, SSD_N, DI), F32)],
        input_output_aliases=aliases,
        compiler_params=_cparams("parallel", "arbitrary"),
        name="ssd_scan_" + ("s" if has_h0 else "p"),
    )(*args)


def _ssd_out_kernel(yf_ref, yb_ref, x2_ref, z_ref, dsk_ref, ng_ref, wo_ref, xres_ref,
                    gate_ref, g2_ref, sc2_ref, sh2_ref, rwT_ref, rb_ref,
                    xn_ref, h2_ref, idx_ref, gcol_ref):
    y = yf_ref[...].astype(F32) + yb_ref[...].astype(F32) + dsk_ref[...] * x2_ref[...].astype(F32)
    y = y * _silu(z_ref[...].astype(F32))
    y = y * lax.rsqrt(jnp.mean(y * y, axis=-1, keepdims=True) + EPS) * ng_ref[...]
    xn = xres_ref[...] + gate_ref[0] * _mm(y.astype(BF16), wo_ref[...])
    xn_ref[...] = xn
    _router_epilogue(xn, g2_ref[...], sc2_ref[0], sh2_ref[0], rwT_ref[...], rb_ref[...],
                     h2_ref, idx_ref, gcol_ref)


def _ssd_out(yf, yb, xbc, z, dskip_x, norm_g, wo, x, router_args):
    full = lambda t: (0, 0)
    rowb = lambda t: (t, 0)
    return pl.pallas_call(
        _ssd_out_kernel,
        grid=(T // TM,),
        in_specs=[pl.BlockSpec((TM, DI), rowb), pl.BlockSpec((TM, DI), rowb),
                  pl.BlockSpec((TM, DI), rowb), pl.BlockSpec((TM, DI), rowb),
                  pl.BlockSpec((1, DI), full), pl.BlockSpec((1, DI), full),
                  pl.BlockSpec((DI, D), full), pl.BlockSpec((TM, D), rowb)] + _router_in_specs(TM, 0),
        out_specs=_router_out_specs(TM, 0),
        out_shape=_ROUTER_OUT_SHAPES,
        compiler_params=_cparams("parallel"),
        name="ssd_out",
    )(yf, yb, xbc, z, dskip_x, norm_g, wo, x, *router_args)


def _gla_in_kernel(x_ref, sh_ref, sc_ref, g_ref, wq_ref, wk_ref, wv_ref, wr_ref, wg_ref, w2_ref, b2_ref,
                   q_ref, k_ref, v_ref, r_ref, gl_ref, *scratch, colmajor):
    h = _modnorm(x_ref[...], g_ref[...], sc_ref[0], sh_ref[0])
    hb = h.astype(BF16)
    glow = _mm(h, wg_ref[...], HI)

    def gate(zdir, cs):
        pre = _mm(glow[:, zdir * GLA_R:(zdir + 1) * GLA_R], w2_ref[zdir, :, cs], HI) + b2_ref[zdir, :, cs]
        return (jnp.minimum(pre, 0.0) - jnp.log1p(jnp.exp(-jnp.abs(pre)))) / GLA_NORM

    prods = [(q_ref, GLA_DK, lambda cs: _mm(hb, wq_ref[:, cs]) * (GLA_HK ** -0.5)),
             (k_ref, GLA_DK, lambda cs: _mm(hb, wk_ref[:, cs])),
             (v_ref, GLA_DV, lambda cs: _mm(hb, wv_ref[:, cs])),
             (r_ref, GLA_DV, lambda cs: _mm(hb, wr_ref[:, cs])),
             (gl_ref, GLA_DK, lambda cs: gate(0, cs)),
             (gl_ref, GLA_DK, lambda cs: gate(1, cs))]
    cn = 256
    if not colmajor:
        for n_out, (o_ref, width, fn) in enumerate(prods):
            base = GLA_DK if n_out == 5 else 0
            for c in range(0, width, cn):
                o_ref[:, base + c:base + c + cn] = fn(slice(c, c + cn)).astype(o_ref.dtype)
    else:
        (scr,) = scratch
        for n_out, (o_ref, width, fn) in enumerate(prods):
            base = GLA_DK if n_out == 5 else 0
            for c in range(0, width, cn):
                val = fn(slice(c, c + cn))
                for jc in range(cn // 128):
                    scr[jc] = val[:, jc * 128:(jc + 1) * 128]
                for jc in range(cn // 128):
                    lo = base + c + jc * 128
                    for col in range(GRID_W):
                        o_ref[0, col, :, lo:lo + 128] = scr[jc, pl.ds(col, 8, stride=GRID_W), :].astype(o_ref.dtype)


def _gla_in(x, shift, scale, g, wq, wk, wv, wr, wg, w2, b2, *, colmajor):
    tm = 8 * GRID_W if colmajor else TM
    off = NP // tm if colmajor else 0
    ntile = (NS if colmajor else NP) // tm
    row = lambda t: (_cond_row(t + off, tm), 0, 0)
    full = lambda t: (0, 0)
    full3 = lambda t: (0, 0, 0)
    widths = [GLA_DK, GLA_DK, GLA_DV, GLA_DV, 2 * GLA_DK]
    dts = [BF16, BF16, BF16, BF16, F32]
    if colmajor:
        out_specs = [pl.BlockSpec((1, GRID_W, 8, w), lambda t: (t // 8, 0, t % 8, 0)) for w in widths]
        out_shape = [jax.ShapeDtypeStruct((NB_S, GRID_W, GRID_W, w), dt) for w, dt in zip(widths, dts)]
        scratch = [pltpu.VMEM((2, tm, 128), F32)]
    else:
        out_specs = [pl.BlockSpec((tm, w), lambda t: (t, 0)) for w in widths]
        out_shape = [jax.ShapeDtypeStruct((NP, w), dt) for w, dt in zip(widths, dts)]
        scratch = []
    return pl.pallas_call(
        functools.partial(_gla_in_kernel, colmajor=colmajor),
        grid=(ntile,),
        in_specs=[pl.BlockSpec((tm, D), lambda t: (t + off, 0)),
                  pl.BlockSpec((1, 1, D), row), pl.BlockSpec((1, 1, D), row),
                  pl.BlockSpec((1, D), full),
                  pl.BlockSpec((D, GLA_DK), full), pl.BlockSpec((D, GLA_DK), full),
                  pl.BlockSpec((D, GLA_DV), full), pl.BlockSpec((D, GLA_DV), full),
                  pl.BlockSpec((D, 2 * GLA_R), full),
                  pl.BlockSpec((2, GLA_R, GLA_DK), full3), pl.BlockSpec((2, 1, GLA_DK), full3)],
        out_specs=out_specs, out_shape=out_shape, scratch_shapes=scratch,
        compiler_params=_cparams("parallel"),
        name="gla_in_" + ("s" if colmajor else "p"),
    )(x, shift, scale, g, wq, wk, wv, wr, wg, w2, b2.reshape(2, 1, GLA_DK))


def _gla_dir_step(q_ref, k_ref, v_ref, gl_ref, o_ref, st_ref, d, rev):
    L = GLA_L
    ii = lax.broadcasted_iota(I32, (L, L), 0)
    jj = lax.broadcasted_iota(I32, (L, L), 1)
    mask = (jj >= ii) if rev else (jj <= ii)
    gk = gl_ref[0]
    g = _mm(mask.astype(F32), gk, HI)
    ri = L - 1 - L // 2 if rev else L // 2
    ei = 0 if rev else L - 1
    gref = g[ri:ri + 1, :]
    gend = g[ei:ei + 1, :]
    qf = q_ref[0].astype(F32)
    kf = k_ref[0].astype(F32)
    qg = (qf * jnp.exp(g - gref)).astype(BF16)
    kg = (kf * jnp.exp(gref - g)).astype(BF16)
    qe = (qf * jnp.exp(g)).astype(BF16)
    ku = (kf * jnp.exp(gend - g)).astype(BF16)
    dec = jnp.exp(gend)
    v = v_ref[0]
    for hh in range(GLA_H):
        ks = slice(hh * GLA_HK, (hh + 1) * GLA_HK)
        vs = slice(hh * GLA_HV, (hh + 1) * GLA_HV)
        att = jnp.where(mask, _nt(qg[:, ks], kg[:, ks]), 0.0)
        vh = v[:, vs]
        st = st_ref[d, hh]
        o = _mm(att.astype(BF16), vh) + _nt(qe[:, ks], st.astype(BF16))
        o_ref[0, :, vs] = o.astype(BF16)
        ut = _mm(vh.astype(F32).T.astype(BF16), ku[:, ks])
        st_ref[d, hh] = st * dec[:, ks] + ut


def _gla_scan_kernel(*refs, has_s0, want_sf, nchunk):
    refs = list(refs)
    fwd, bwd = refs[:4], refs[4:8]
    pos = 8
    s0_ref = None
    if has_s0:
        s0_ref = refs[pos]
        pos += 1
    of_ref, ob_ref = refs[pos], refs[pos + 1]
    pos += 2
    sf_ref = None
    if want_sf:
        sf_ref = refs[pos]
        pos += 1
    st_ref = refs[pos]
    c = pl.program_id(1)

    @pl.when(c == 0)
    def _init():
        for d in range(2):
            for hh in range(GLA_H):
                if has_s0:
                    st_ref[d, hh] = s0_ref[0, d, hh].T
                else:
                    st_ref[d, hh] = jnp.zeros((GLA_HV, GLA_HK), F32)

    _gla_dir_step(*fwd, of_ref, st_ref, 0, False)
    _gla_dir_step(*bwd, ob_ref, st_ref, 1, True)

    if want_sf:
        @pl.when(c == nchunk - 1)
        def _fin():
            for d in range(2):
                for hh in range(GLA_H):
                    sf_ref[0, d, hh] = st_ref[d, hh].T


def _gla_scan(q, k, v, gl, s0, *, nb, nchunk, want_sf):
    has_s0 = s0 is not None
    fidx = lambda b, c: b * nchunk + c
    bidx = lambda b, c: b * nchunk + (nchunk - 1 - c)
    blk = lambda w, f: pl.BlockSpec((1, GLA_L, w), lambda b, c: (f(b, c), 0, 0))
    in_specs = [blk(GLA_DK, fidx), blk(GLA_DK, fidx), blk(GLA_DV, fidx),
                pl.BlockSpec((1, GLA_L, GLA_DK), lambda b, c: (fidx(b, c), 0, 0)),
                blk(GLA_DK, bidx), blk(GLA_DK, bidx), blk(GLA_DV, bidx),
                pl.BlockSpec((1, GLA_L, GLA_DK), lambda b, c: (bidx(b, c), 0, 1))]
    args = [t.reshape(nb * nchunk, GLA_L, t.shape[-1]) for t in (q, k, v, gl)] * 2
    sshape = (1, 2, GLA_H, GLA_HK, GLA_HV)
    if has_s0:
        in_specs.append(pl.BlockSpec(sshape, lambda b, c: (b, 0, 0, 0, 0)))
        args.append(s0)
    out_specs = [blk(GLA_DV, fidx), blk(GLA_DV, bidx)]
    out_shape = [jax.ShapeDtypeStruct((nb * nchunk, GLA_L, GLA_DV), BF16)] * 2
    if want_sf:
        out_specs.append(pl.BlockSpec(sshape, lambda b, c: (b, 0, 0, 0, 0)))
        out_shape.append(jax.ShapeDtypeStruct((nb,) + sshape[1:], F32))
    return pl.pallas_call(
        functools.partial(_gla_scan_kernel, has_s0=has_s0, want_sf=want_sf, nchunk=nchunk),
        grid=(nb, nchunk),
        in_specs=in_specs, out_specs=out_specs, out_shape=out_shape,
        scratch_shapes=[pltpu.VMEM((2, GLA_H, GLA_HV, GLA_HK), F32)],
        compiler_params=_cparams("parallel", "arbitrary"),
        name="gla_scan_" + ("s" if has_s0 else "p"),
    )(*args)


def _gla_out_kernel(of_ref, ob_ref, r_ref, ng_ref, wo_ref, xres_ref,
                    gate_ref, g2_ref, sc2_ref, sh2_ref, rwT_ref, rb_ref, *rest, colmajor):
    if colmajor:
        (_, _, _, _, xn_ref, h2_ref, idx_ref, gcol_ref, so_ref, sr_ref) = rest
        nc = GLA_DV // 128
        for col in range(GRID_W):
            osum = of_ref[0, col].astype(F32) + ob_ref[0, col].astype(F32)
            rcol = r_ref[0, col].astype(F32)
            for jc in range(nc):
                so_ref[jc, pl.ds(col, 8, stride=GRID_W), :] = osum[:, jc * 128:(jc + 1) * 128]
                sr_ref[jc, pl.ds(col, 8, stride=GRID_W), :] = rcol[:, jc * 128:(jc + 1) * 128]
        o = jnp.concatenate([so_ref[jc] for jc in range(nc)], axis=1)
        r = jnp.concatenate([sr_ref[jc] for jc in range(nc)], axis=1)
    else:
        xn_ref, h2_ref, idx_ref, gcol_ref = rest
        o = of_ref[...].astype(F32) + ob_ref[...].astype(F32)
        r = r_ref[...].astype(F32)
    parts = []
    for hh in range(GLA_H):
        oh = o[:, hh * GLA_HV:(hh + 1) * GLA_HV]
        parts.append(oh * lax.rsqrt(jnp.mean(oh * oh, axis=-1, keepdims=True) + EPS) * ng_ref[...])
    y = jnp.concatenate(parts, axis=1) * _silu(r)
    xn = xres_ref[...] + gate_ref[0] * _mm(y.astype(BF16), wo_ref[...])
    xn_ref[...] = xn
    _router_epilogue(xn, g2_ref[...], sc2_ref[0], sh2_ref[0], rwT_ref[...], rb_ref[...],
                     h2_ref, idx_ref, gcol_ref)


def _gla_out(of, ob, r, norm_g, wo, x, router_args, prev, *, colmajor):
    tm = 8 * GRID_W if colmajor else TM
    off = NP // tm if colmajor else 0
    ntile = (NS if colmajor else NP) // tm
    full = lambda t: (0, 0)
    if colmajor:
        blk = lambda: pl.BlockSpec((1, GRID_W, 8, GLA_DV), lambda t: (t // 8, 0, t % 8, 0))
        scratch = [pltpu.VMEM((GLA_DV // 128, tm, 128), F32), pltpu.VMEM((GLA_DV // 128, tm, 128), F32)]
    else:
        blk = lambda: pl.BlockSpec((tm, GLA_DV), lambda t: (t, 0))
        scratch = []
    in_specs = ([blk(), blk(), blk(), pl.BlockSpec((1, GLA_HV), full), pl.BlockSpec((GLA_DV, D), full),
                 pl.BlockSpec((tm, D), lambda t: (t + off, 0))] + _router_in_specs(tm, off))
    args = [of, ob, r, norm_g, wo, x, *router_args]
    aliases = {}
    if prev is not None:
        for i, p in enumerate(prev):
            in_specs.append(pl.BlockSpec(memory_space=pl.ANY))
            args.append(p)
            aliases[len(args) - 1] = i
    return pl.pallas_call(
        functools.partial(_gla_out_kernel, colmajor=colmajor),
        grid=(ntile,),
        in_specs=in_specs, out_specs=_router_out_specs(tm, off), out_shape=_ROUTER_OUT_SHAPES,
        scratch_shapes=scratch, input_output_aliases=aliases,
        compiler_params=_cparams("parallel"),
        name="gla_out_" + ("s" if colmajor else "p"),
    )(*args)


def _sc_kernel(x_ref, sh_ref, sc_ref, g_ref, wb_ref, wc_ref, wu_ref, cw_ref, wo_ref,
               gate_ref, g2_ref, sc2_ref, sh2_ref, rwT_ref, rb_ref,
               xn_ref, h2_ref, idx_ref, gcol_ref, y_scr):
    t = pl.program_id(0)
    x = x_ref[...]
    hb = _modnorm(x, g_ref[...], sc_ref[0], sh_ref[0]).astype(BF16)
    segm1 = jnp.where(t < NP // TM, SEQ_P - 1, GRID_W - 1)
    cn = 512
    for j in range(0, D, cn):
        js = slice(j, j + cn)
        gcu = _mm(hb, wc_ref[:, js]) * _mm(hb, wu_ref[:, js])
        y_scr[:, js] = (_mm(hb, wb_ref[:, js]) * _seg_conv(gcu, cw_ref[:, js], 3, segm1)).astype(BF16)
    xn = x + gate_ref[0] * _mm(y_scr[...], wo_ref[...])
    xn_ref[...] = xn
    _router_epilogue(xn, g2_ref[...], sc2_ref[0], sh2_ref[0], rwT_ref[...], rb_ref[...],
                     h2_ref, idx_ref, gcol_ref)


def _shortconv(x, shift, scale, g, wb, wc, wu, conv_w, wo, router_args):
    row = lambda t: (_cond_row(t, TM), 0, 0)
    full = lambda t: (0, 0)
    return pl.pallas_call(
        _sc_kernel,
        grid=(T // TM,),
        in_specs=[pl.BlockSpec((TM, D), lambda t: (t, 0)),
                  pl.BlockSpec((1, 1, D), row), pl.BlockSpec((1, 1, D), row),
                  pl.BlockSpec((1, D), full),
                  pl.BlockSpec((D, D), full), pl.BlockSpec((D, D), full), pl.BlockSpec((D, D), full),
                  pl.BlockSpec((3, D), full), pl.BlockSpec((D, D), full)] + _router_in_specs(TM, 0),
        out_specs=_router_out_specs(TM, 0), out_shape=_ROUTER_OUT_SHAPES,
        scratch_shapes=[pltpu.VMEM((TM, D), BF16)],
        compiler_params=_cparams("parallel"),
        name="shortconv",
    )(x, shift, scale, g, wb, wc, wu, conv_w, wo, *router_args)


def _to_rows(tile_ref, m):
    return jnp.concatenate([tile_ref[pl.ds(s, m, stride=SUB), :] for s in range(SUB)], axis=1)


def _store_tiles(tile_ref, val, m):
    for s in range(SUB):
        tile_ref[pl.ds(s, m, stride=SUB), :] = val[:, s * 128:(s + 1) * 128]


def _moe_kernel(n_ref, ie_ref, chg_ref,
                tok0_ref, tok1_ref, tokn_ref, dstp_ref, dstc_ref,
                h_hbm, wgu_ref, bgu_ref, wd_ref, bd_ref, y_hbm,
                xbuf, obuf, act_scr, wgu_bf, wd_bf, gsem, ssem):
    i = pl.program_id(0)
    n = n_ref[0]
    slot = lax.rem(i, NSLOT)
    nslot = lax.rem(i + 2, NSLOT)

    def gather(idx_ref, r, s):
        src = h_hbm.at[pl.ds(pl.multiple_of(idx_ref[0, 0, r], SUB), SUB), :]
        return pltpu.make_async_copy(src, xbuf.at[s, pl.ds(pl.multiple_of(r * SUB, SUB), SUB), :], gsem.at[s])

    def scatter(idx_ref, r, s):
        dst = y_hbm.at[pl.ds(pl.multiple_of(idx_ref[0, 0, r], SUB), SUB), :]
        return pltpu.make_async_copy(obuf.at[s, pl.ds(pl.multiple_of(r * SUB, SUB), SUB), :], dst, ssem.at[s])

    def wait_gather(s):
        pltpu.make_async_copy(h_hbm.at[pl.ds(0, BM * SUB), :], xbuf.at[s], gsem.at[s]).wait()

    def wait_scatter(s):
        pltpu.make_async_copy(obuf.at[s], y_hbm.at[pl.ds(0, BM * SUB), :], ssem.at[s]).wait()

    @pl.when((i < n) & (chg_ref[i] == 1))
    def _cast_weights():
        wgu_bf[...] = wgu_ref[0, 0].astype(BF16)
        wd_bf[...] = wd_ref[0, 0].astype(BF16)

    @pl.when(i == 0)
    def _prologue():
        obuf[NSLOT - 1] = jnp.zeros((BM * SUB, 128), F32)

        def body(r, c):
            gather(tok0_ref, r, 0).start()
            gather(tok1_ref, r, 1).start()
            return c
        lax.fori_loop(0, BM, body, 0)

    @pl.when((i >= 2) & (i < n))
    def _free_obuf():
        wait_scatter(slot)

    @pl.when(i < n)
    def _main():
        wait_gather(slot)
        xb = _to_rows(xbuf.at[slot], BM).astype(BF16)
        half = DFF // 2
        per = BM // 4
        for q in range(4):
            for r in range(q * per, (q + 1) * per):
                gather(tokn_ref, r, nslot).start(priority=r % 2)
                scatter(dstp_ref, r, nslot).start(priority=(r + 1) % 2)
            if q < 2:
                cs = slice(q * half, (q + 1) * half)
                us = slice(DFF + q * half, DFF + (q + 1) * half)
                gg = jnp.minimum(_mm(xb, wgu_bf[:, cs]) + bgu_ref[0, 0, :, cs], LIMIT)
                uu = jnp.clip(_mm(xb, wgu_bf[:, us]) + bgu_ref[0, 0, :, us], -LIMIT, LIMIT)
                act_scr[:, cs] = (gg * _sigmoid(ALPHA * gg) * (uu + 1.0)).astype(BF16)
            else:
                cs = slice((q - 2) * half, (q - 1) * half)
                yo = _mm(act_scr[...], wd_bf[:, cs]) + bd_ref[0, 0, :, cs]
                for s in range(SUB // 2):
                    sg = (q - 2) * (SUB // 2) + s
                    obuf[slot, pl.ds(sg, BM, stride=SUB), :] = yo[:, s * 128:(s + 1) * 128]

    @pl.when(i == n - 1)
    def _drain():
        pslot = lax.rem(i + 1, NSLOT)
        wait_scatter(pslot)
        wait_scatter(nslot)

        def body(r, c):
            scatter(dstc_ref, r, slot).start()
            return c
        lax.fori_loop(0, BM, body, 0)
        wait_scatter(slot)
        wait_gather(pslot)
        wait_gather(nslot)


def _moe_experts(plan, h2t, w_gu, b_gu, w_down, b_down, layer):
    n, ie, chg, tok8, dst8 = plan
    smem = lambda f: pl.BlockSpec((1, 1, BM), f, memory_space=pltpu.SMEM)
    wspec = lambda shape: pl.BlockSpec(shape, lambda i, n, ie, chg: (layer, ie[i], 0, 0))
    grid_spec = pltpu.PrefetchScalarGridSpec(
        num_scalar_prefetch=3,
        grid=(NB_TOT,),
        in_specs=[smem(lambda i, *_: (0, 0, 0)),
                  smem(lambda i, *_: (1, 0, 0)),
                  smem(lambda i, *_: (i + 2, 0, 0)),
                  smem(lambda i, *_: (i, 0, 0)),
                  smem(lambda i, *_: (i + 1, 0, 0)),
                  pl.BlockSpec(memory_space=pl.ANY),
                  wspec((1, 1, D, 2 * DFF)), wspec((1, 1, 1, 2 * DFF)),
                  wspec((1, 1, DFF, D)), wspec((1, 1, 1, D))],
        out_specs=pl.BlockSpec(memory_space=pl.ANY),
        scratch_shapes=[pltpu.VMEM((NSLOT, BM * SUB, 128), F32), pltpu.VMEM((NSLOT, BM * SUB, 128), F32),
                        pltpu.VMEM((BM, DFF), BF16),
                        pltpu.VMEM((D, 2 * DFF), BF16), pltpu.VMEM((DFF, D), BF16),
                        pltpu.SemaphoreType.DMA((NSLOT,)), pltpu.SemaphoreType.DMA((NSLOT,))],
    )
    return pl.pallas_call(
        _moe_kernel,
        grid_spec=grid_spec,
        out_shape=jax.ShapeDtypeStruct((Y_ROWS * SUB, 128), F32),
        compiler_params=_cparams("arbitrary"),
        name="moe_experts",
    )(n, ie, chg, tok8, tok8, tok8, dst8, dst8, h2t, w_gu, b_gu.reshape(DEPTH, NE, 1, 2 * DFF),
      w_down, b_down.reshape(DEPTH, NE, 1, D))


def _moe_plan(idx_t):
    e_flat = idx_t.reshape(NAS)
    abits = (NAS - 1).bit_length()
    order = jnp.sort((e_flat << abits) | jnp.arange(NAS, dtype=I32)) & ((1 << abits) - 1)
    counts = jnp.sum((e_flat[None, :] == jnp.arange(NE, dtype=I32)[:, None]).astype(I32), axis=1)
    nblk_e = (counts + BM - 1) // BM
    bend = jnp.cumsum(nblk_e).astype(I32)
    bstart = bend - nblk_e
    off_end = jnp.cumsum(counts).astype(I32)
    off = off_end - counts
    n = bend[-1]
    blk = jnp.arange(NB_TOT + 2, dtype=I32)
    be = jnp.minimum(jnp.sum((blk[:, None] >= bend[None, :]).astype(I32), axis=1), NE - 1)
    real = blk < n
    r = jnp.arange(BM, dtype=I32)[None, :]
    p = (off[be] + (blk - bstart[be]) * BM)[:, None] + r
    valid = real[:, None] & (p < off_end[be][:, None])
    a = order[jnp.clip(p, 0, NAS - 1)]
    tok8 = jnp.where(valid, (a % T) * SUB, 0)
    pad8 = (NAS + (blk % 2)[:, None] * BM + r) * SUB
    dst8 = jnp.where(valid, a * SUB, pad8)
    dst8 = jnp.concatenate([(NAS + BM + r) * SUB, dst8[:NB_TOT]], axis=0)
    ie = jnp.where(real, be, be[jnp.maximum(n - 1, 0)])[:NB_TOT]
    chg = (ie != jnp.concatenate([jnp.full((1,), -1, I32), ie[:-1]])).astype(I32)
    return (n.reshape(1), ie, chg, tok8.reshape(NB_TOT + 2, 1, BM), dst8.reshape(NB_TOT + 1, 1, BM))


def _combine_kernel(x_ref, y0_ref, y1_ref, y2_ref, y3_ref, gcol_ref, gate_ref, *rest, final):
    if final:
        fg_ref, o_ref = rest
    else:
        (o_ref,) = rest
    gc = gcol_ref[...]
    m = x_ref.shape[0]
    acc = _to_rows(y0_ref, m) * gc[:, 0:1]
    for k, y_ref in enumerate((y1_ref, y2_ref, y3_ref), start=1):
        acc = acc + _to_rows(y_ref, m) * gc[:, k:k + 1]
    xn = x_ref[...] + gate_ref[0] * acc
    if final:
        xn = xn * lax.rsqrt(jnp.mean(xn * xn, axis=-1, keepdims=True) + EPS) * fg_ref[...]
    o_ref[...] = xn


def _combine(x, y4, gcol, gate2, final_g):
    final = final_g is not None
    nt = T // TM
    yspec = lambda k: pl.BlockSpec((TM * SUB, 128), lambda t: (k * nt + t, 0))
    in_specs = [pl.BlockSpec((TM, D), lambda t: (t, 0)),
                yspec(0), yspec(1), yspec(2), yspec(3),
                pl.BlockSpec((TM, 128), lambda t: (t, 0)),
                pl.BlockSpec((1, 1, D), lambda t: (_cond_row(t, TM), 0, 0))]
    args = [x, y4, y4, y4, y4, gcol, gate2]
    if final:
        in_specs.append(pl.BlockSpec((1, D), lambda t: (0, 0)))
        args.append(final_g)
    return pl.pallas_call(
        functools.partial(_combine_kernel, final=final),
        grid=(nt,),
        in_specs=in_specs,
        out_specs=pl.BlockSpec((TM, D), lambda t: (t, 0)),
        out_shape=jax.ShapeDtypeStruct((T, D), F32),
        compiler_params=_cparams("parallel"),
        name="moe_combine",
    )(*args)


def kernel(x_prompt, x_sample, state_ssd, state_gla, c, c_ctx, w_mod, b_mod, norm1_g, norm2_g, ssd_w_in, ssd_conv_w, ssd_conv_b, ssd_dt_bias, ssd_a_log, ssd_d, ssd_norm_g, ssd_w_out, gla_w_in, gla_w_gate2, gla_b_gate2, gla_norm_g, gla_w_out, sc_w_in, sc_conv_w, sc_w_out, router_w, router_b, moe_w_gu, moe_b_gu, moe_w_down, moe_b_down, final_norm_g):
    x = jnp.concatenate([x_prompt.reshape(NP, D), x_sample.reshape(NS, D)], axis=0)
    cond8 = jnp.concatenate([c_ctx[None, :], c, jnp.zeros((8 - 1 - NB_S, D), F32)], axis=0)
    mod = _modulation(cond8, w_mod, b_mod)

    new_ssd, new_gla = [], []
    for i in range(DEPTH):
        m6 = [mod[i, :, j * D:(j + 1) * D].reshape(8, 1, D) for j in range(6)]
        shift1, scale1, gate1, shift2, scale2, gate2 = m6
        n1 = norm1_g[i].reshape(1, D)
        router_args = (gate1, norm2_g[i].reshape(1, D), scale2, shift2,
                       router_w[i].T, router_b[i].reshape(NE, 1))
        kind, j = i % 3, i // 3
        if kind == 0:
            w_in = ssd_w_in[j]
            wz = w_in[:, :DI].astype(BF16)
            wx = w_in[:, DI:DI + SSD_XBC].astype(BF16)
            wdt = w_in[:, DI + SSD_XBC:]
            z, xbc, dt, dtT = _ssd_in(x, shift1, scale1, n1, wz, wx, wdt, ssd_conv_w[j], ssd_conv_b[j],
                                      ssd_dt_bias[j].reshape(-1))
            a = -jnp.exp(ssd_a_log[j].astype(F32))
            yfp, ybp, hf = _ssd_scan(xbc, dt, dtT, a, None, None, nb=NB_P, nchunk=SEQ_P // SSD_L,
                                     base=0, want_hf=True)
            ys = _ssd_scan(xbc, dt, dtT, a, state_ssd[:, j].reshape(NB_S, 2, DI, SSD_N), (yfp, ybp),
                           nb=NB_S, nchunk=SEQ_S // SSD_L, base=NP, want_hf=False)
            new_ssd.append(hf.reshape(NB_P, 2, SSD_H, SSD_P, SSD_N))
            dskip_x = jnp.repeat(ssd_d[j].astype(F32), SSD_P).reshape(1, DI)
            x, h2, idx_t, gcol = _ssd_out(ys[0], ys[1], xbc, z, dskip_x, ssd_norm_g[j].reshape(1, DI),
                                          ssd_w_out[j].astype(BF16), x, router_args)
        elif kind == 1:
            w_in = gla_w_in[j]
            wq = w_in[:, :GLA_DK].astype(BF16)
            wk = w_in[:, GLA_DK:2 * GLA_DK].astype(BF16)
            wv = w_in[:, 2 * GLA_DK:2 * GLA_DK + GLA_DV].astype(BF16)
            wr = w_in[:, 2 * GLA_DK + GLA_DV:2 * GLA_DK + 2 * GLA_DV].astype(BF16)
            wg = w_in[:, 2 * GLA_DK + 2 * GLA_DV:]
            wo = gla_w_out[j].astype(BF16)
            ng = gla_norm_g[j].reshape(1, GLA_HV)
            prev = None
            for colmajor in (False, True):
                q, k, v, r, gl = _gla_in(x, shift1, scale1, n1, wq, wk, wv, wr, wg,
                                         gla_w_gate2[j], gla_b_gate2[j], colmajor=colmajor)
                nb = NB_S if colmajor else NB_P
                nchunk = (SEQ_S if colmajor else SEQ_P) // GLA_L
                res = _gla_scan(q, k, v, gl, state_gla[:, j] if colmajor else None,
                                nb=nb, nchunk=nchunk, want_sf=not colmajor)
                oshape = (NB_S, GRID_W, GRID_W, GLA_DV) if colmajor else (NP, GLA_DV)
                os_ = [res[0].reshape(oshape), res[1].reshape(oshape)]
                if not colmajor:
                    new_gla.append(res[2])
                prev = _gla_out(os_[0], os_[1], r, ng, wo, x, router_args, prev, colmajor=colmajor)
            x, h2, idx_t, gcol = prev
        else:
            w_in = sc_w_in[j]
            x, h2, idx_t, gcol = _shortconv(x, shift1, scale1, n1, w_in[:, :D].astype(BF16),
                                            w_in[:, D:2 * D].astype(BF16), w_in[:, 2 * D:].astype(BF16),
                                            sc_conv_w[j], sc_w_out[j].astype(BF16), router_args)
        y4 = _moe_experts(_moe_plan(idx_t), h2, moe_w_gu, moe_b_gu, moe_w_down, moe_b_down, i)
        x = _combine(x, y4, gcol, gate2, final_norm_g.reshape(1, D) if i == DEPTH - 1 else None)

    y_prompt = x[:NP].reshape(NB_P, SEQ_P, D)
    y_sample = x[NP:].reshape(NB_S, SEQ_S, D)
    new_state_ssd = jnp.stack(new_ssd, axis=1)
    new_state_gla = jnp.stack(new_gla, axis=1)
    return (y_prompt, y_sample, new_state_ssd, new_state_gla)
```
